```python
import jax, jax.numpy as jnp
from jax import lax
import numpy as np

D_MODEL = 2048
BATCH = 8
SEQ = 4096
DEPTH = 2

CHUNK = 64
Q_BLOCK = 128
D_PLE = 256
D_FF = 4 * D_MODEL
LN_EPS = 1e-5
ATT_HEADS = 8
ATT_HEAD_DIM = 128
ATT_WIDTH = ATT_HEADS * ATT_HEAD_DIM
CONV_CH = D_MODEL - ATT_WIDTH
CONV_WIDTH = 31
EVEN_IN = 3 * ATT_WIDTH + ATT_HEADS + 2 * CONV_CH
ML_HEADS = 8
ML_QK_DIM = 128
ML_V_DIM = D_MODEL // ML_HEADS
ML_QK_WIDTH = ML_HEADS * ML_QK_DIM
ML_V_WIDTH = ML_HEADS * ML_V_DIM
ODD_IN = 2 * ML_QK_WIDTH + 2 * ML_V_WIDTH + 2 * ML_HEADS
N_EVEN = (DEPTH + 1) // 2
N_ODD = DEPTH // 2
ALPHA = (2 * DEPTH) ** 0.25
BETA = (8 * DEPTH) ** -0.25

kernel_name = "hybrid_fox_conformer_mlstm_deepnorm"


def layer_norm(x, g, b):
    xf = x.astype(jnp.float32)
    mu = jnp.mean(xf, axis=-1, keepdims=True)
    xc = xf - mu
    var = jnp.mean(jnp.square(xc), axis=-1, keepdims=True)
    y = xc * lax.rsqrt(var + LN_EPS) * g.astype(jnp.float32) + b.astype(jnp.float32)
    return y.astype(x.dtype)


def split_heads(t, n_heads):
    B, S, W = t.shape
    return t.reshape(B, S, n_heads, W // n_heads).transpose(0, 2, 1, 3)


def forgetting_attention(q, k, v, f_logit):
    B, H, S, Dh = q.shape
    nb = S // Q_BLOCK
    scale = Dh ** -0.5
    log_f = jax.nn.log_sigmoid(f_logit.astype(jnp.float32))
    F = jnp.cumsum(log_f, axis=1).transpose(0, 2, 1)
    qb = jnp.moveaxis(q.reshape(B, H, nb, Q_BLOCK, Dh), 2, 0)
    Fq = jnp.moveaxis(F.reshape(B, H, nb, Q_BLOCK), 2, 0)
    q_pos = jnp.arange(S, dtype=jnp.int32).reshape(nb, Q_BLOCK)
    k_pos = jnp.arange(S, dtype=jnp.int32)

    def block(args):
        qi, Fi, pi = args
        s = jnp.einsum('bhqd,bhkd->bhqk', qi, k, preferred_element_type=jnp.float32) * scale
        s = s + Fi[..., :, None] - F[:, :, None, :]
        s = jnp.where(k_pos[None, :] <= pi[:, None], s, -jnp.inf)
        pr = jax.nn.softmax(s, axis=-1)
        return jnp.einsum('bhqk,bhkd->bhqd', pr.astype(v.dtype), v)

    out = lax.map(block, (qb, Fq, q_pos))
    return out.transpose(1, 0, 3, 2, 4).reshape(B, S, H * Dh)


def conformer_conv(u, dw_kernel, dw_bias, norm_g, norm_b):
    a, gate = jnp.split(u, 2, axis=-1)
    y = a * jax.nn.sigmoid(gate)
    C = y.shape[-1]
    y = lax.conv_general_dilated(
        y, dw_kernel[:, None, :].astype(y.dtype), window_strides=(1,),
        padding=((CONV_WIDTH - 1, 0),), dimension_numbers=('NWC', 'WIO', 'NWC'),
        feature_group_count=C) + dw_bias.astype(y.dtype)
    y = layer_norm(y, norm_g, norm_b)
    return jax.nn.silu(y)


def mlstm_chunkwise(q, k, v, i_logit, f_logit):
    B, H, S, dk = q.shape
    dv = v.shape[-1]
    nc = S // CHUNK
    f32 = jnp.float32
    q = q.astype(f32)
    k = k.astype(f32) * (dk ** -0.5)
    v = v.astype(f32)
    log_f = jax.nn.log_sigmoid(f_logit.astype(f32))
    i_pre = i_logit.astype(f32)

    def to_chunks(t):
        return jnp.moveaxis(t.reshape(B, H, nc, CHUNK, *t.shape[3:]), 2, 0)

    tri = jnp.tril(jnp.ones((CHUNK, CHUNK), dtype=bool))

    def step(carry, xs):
        C, n, m = carry
        qc, kc, vc, ic, fc = xs
        b = jnp.cumsum(fc, axis=-1)
        d_intra = jnp.where(tri, b[..., :, None] - b[..., None, :] + ic[..., None, :], -jnp.inf)
        d_inter = b + m[..., None]
        m_t = jnp.maximum(d_inter, jnp.max(d_intra, axis=-1))
        w_intra = jnp.exp(d_intra - m_t[..., None])
        w_inter = jnp.exp(d_inter - m_t)
        s = jnp.einsum('bhtd,bhsd->bhts', qc, kc) * w_intra
        num = (w_inter[..., None] * jnp.einsum('bhtd,bhde->bhte', qc, C)
               + jnp.einsum('bhts,bhse->bhte', s, vc))
        den = w_inter * jnp.einsum('bhtd,bhd->bht', qc, n) + jnp.sum(s, axis=-1)
        h = num / jnp.maximum(jnp.abs(den), jnp.exp(-m_t))[..., None]
        b_last = b[..., -1]
        g = b_last[..., None] - b + ic
        m_new = jnp.maximum(b_last + m, jnp.max(g, axis=-1))
        decay = jnp.exp(b_last + m - m_new)
        wk = jnp.exp(g - m_new[..., None])[..., None] * kc
        C_new = decay[..., None, None] * C + jnp.einsum('bhsd,bhse->bhde', wk, vc)
        n_new = decay[..., None] * n + jnp.sum(wk, axis=2)
        return (C_new, n_new, m_new), h

    init = (jnp.zeros((B, H, dk, dv), f32), jnp.zeros((B, H, dk), f32), jnp.zeros((B, H), f32))
    xs = (to_chunks(q), to_chunks(k), to_chunks(v), to_chunks(i_pre), to_chunks(log_f))
    _, h = lax.scan(step, init, xs)
    return jnp.moveaxis(h, 0, 2).reshape(B, H, S, dv).transpose(0, 2, 1, 3).reshape(B, S, H * dv)


def even_mixer(x, w_in, b_fgate, dw_kernel, dw_bias, cnorm_g, cnorm_b, w_out):
    z = x @ w_in
    q, k, v, f_pre, u = jnp.split(
        z, (ATT_WIDTH, 2 * ATT_WIDTH, 3 * ATT_WIDTH, 3 * ATT_WIDTH + ATT_HEADS), axis=-1)
    attn = forgetting_attention(split_heads(q, ATT_HEADS), split_heads(k, ATT_HEADS),
                                split_heads(v, ATT_HEADS), f_pre + b_fgate)
    conv = conformer_conv(u, dw_kernel, dw_bias, cnorm_g, cnorm_b)
    return jnp.concatenate([attn.astype(x.dtype), conv.astype(x.dtype)], axis=-1) @ w_out


def odd_mixer(x, w_in, b_igate, b_fgate, w_out):
    z = x @ w_in
    o1 = 2 * ML_QK_WIDTH + ML_V_WIDTH
    q, k, v, i_pre, f_pre, o_pre = jnp.split(
        z, (ML_QK_WIDTH, 2 * ML_QK_WIDTH, o1, o1 + ML_HEADS, o1 + 2 * ML_HEADS), axis=-1)
    h = mlstm_chunkwise(split_heads(q, ML_HEADS), split_heads(k, ML_HEADS), split_heads(v, ML_HEADS),
                        (i_pre + b_igate).transpose(0, 2, 1), (f_pre + b_fgate).transpose(0, 2, 1))
    return (jax.nn.sigmoid(o_pre) * h.astype(x.dtype)) @ w_out


def setup_inputs(seed: int = 0) -> dict:
    key = jax.random.key(seed)
    ks = jax.random.split(key, 24)
    nrm = jax.random.normal
    f32 = jnp.float32
    D = D_MODEL
    return {
        "x": nrm(ks[0], (BATCH, SEQ, D), f32),
        "p": nrm(ks[1], (DEPTH, BATCH, SEQ, D_PLE), f32),
        "ev_w_in": nrm(ks[2], (N_EVEN, D, EVEN_IN), f32) * D ** -0.5,
        "ev_b_fgate": 2.0 + 0.1 * nrm(ks[3], (N_EVEN, ATT_HEADS), f32),
        "ev_dw_kernel": nrm(ks[4], (N_EVEN, CONV_WIDTH, CONV_CH), f32) * CONV_WIDTH ** -0.5,
        "ev_dw_bias": 0.02 * nrm(ks[5], (N_EVEN, CONV_CH), f32),
        "ev_cnorm_g": 1.0 + 0.02 * nrm(ks[6], (N_EVEN, CONV_CH), f32),
        "ev_cnorm_b": 0.02 * nrm(ks[7], (N_EVEN, CONV_CH), f32),
        "ev_w_out": nrm(ks[8], (N_EVEN, D, D), f32) * (D ** -0.5 * BETA),
        "od_w_in": nrm(ks[9], (N_ODD, D, ODD_IN), f32) * D ** -0.5,
        "od_b_igate": 0.1 * nrm(ks[10], (N_ODD, ML_HEADS), f32),
        "od_b_fgate": 3.0 + 0.1 * nrm(ks[11], (N_ODD, ML_HEADS), f32),
        "od_w_out": nrm(ks[12], (N_ODD, D, D), f32) * (D ** -0.5 * BETA),
        "ln_mix_g": 1.0 + 0.02 * nrm(ks[13], (DEPTH, D), f32),
        "ln_mix_b": 0.02 * nrm(ks[14], (DEPTH, D), f32),
        "w_up": nrm(ks[15], (DEPTH, D, D_FF), f32) * D ** -0.5,
        "w_down": nrm(ks[16], (DEPTH, D_FF, D), f32) * (D_FF ** -0.5 * BETA),
        "ln_ffn_g": 1.0 + 0.02 * nrm(ks[17], (DEPTH, D), f32),
        "ln_ffn_b": 0.02 * nrm(ks[18], (DEPTH, D), f32),
        "w_ple": nrm(ks[19], (DEPTH, D_PLE, D), f32) * D_PLE ** -0.5,
        "w_ple_gate": nrm(ks[20], (DEPTH, D, D), f32) * D ** -0.5,
    }


def reference(x, p, ev_w_in, ev_b_fgate, ev_dw_kernel, ev_dw_bias, ev_cnorm_g, ev_cnorm_b, ev_w_out,
              od_w_in, od_b_igate, od_b_fgate, od_w_out, ln_mix_g, ln_mix_b, w_up, w_down,
              ln_ffn_g, ln_ffn_b, w_ple, w_ple_gate):
    for i in range(DEPTH):
        j = i // 2
        if i % 2 == 0:
            y = even_mixer(x, ev_w_in[j], ev_b_fgate[j], ev_dw_kernel[j], ev_dw_bias[j],
                           ev_cnorm_g[j], ev_cnorm_b[j], ev_w_out[j])
        else:
            y = odd_mixer(x, od_w_in[j], od_b_igate[j], od_b_fgate[j], od_w_out[j])
        x = layer_norm(ALPHA * x + y, ln_mix_g[i], ln_mix_b[i])
        hid = jnp.square(jax.nn.relu(x @ w_up[i]))
        x = layer_norm(ALPHA * x + hid @ w_down[i], ln_ffn_g[i], ln_ffn_b[i])
        x = x + jax.nn.sigmoid(x @ w_ple_gate[i]) * (p[i] @ w_ple[i])
    return x
```

```python
import functools

import jax
import jax.numpy as jnp
from jax import lax
from jax.experimental import pallas as pl
from jax.experimental.pallas import tpu as pltpu

F32 = jnp.float32
BF16 = jnp.bfloat16

D_MODEL = 2048
DEPTH = 2
D_PLE = 256
D_FF = 4 * D_MODEL
LN_EPS = 1e-5
HEADS = 8
QK_DIM = 128
ATT_WIDTH = HEADS * QK_DIM
CONV_CH = D_MODEL - ATT_WIDTH
CONV_WIDTH = 31
ML_V_DIM = D_MODEL // HEADS
ALPHA = (2 * DEPTH) ** 0.25
QK_SCALE = QK_DIM ** -0.5

LANES = 128
SUBLANES = 8
GATE_PAD = LANES
CONV_HALO = 32
VMEM_LIMIT = 58 * 1024 * 1024

PROJ_TM, PROJ_TN = 1024, 512
ATT_TQ = 512
CONV_TS = 256
OUT_TM = 512
FFN_TM, FFN_TF = 512, 512
PLE_TM = 512
ML_CHUNK = 256


def _params(*sem):
    return pltpu.CompilerParams(dimension_semantics=sem, vmem_limit_bytes=VMEM_LIMIT)


def _layer_norm(t, g, b):
    mu = jnp.mean(t, axis=-1, keepdims=True)
    tc = t - mu
    var = jnp.mean(tc * tc, axis=-1, keepdims=True)
    return tc * lax.rsqrt(var + LN_EPS) * g + b


def _log_sigmoid(x):
    return -(jnp.maximum(-x, 0.0) + jnp.log1p(jnp.exp(-jnp.abs(x))))


def _sigmoid(x):
    return 1.0 / (1.0 + jnp.exp(-x))


def _split3(x):
    hi = x.astype(BF16)
    r1 = x - hi.astype(F32)
    mid = r1.astype(BF16)
    lo = (r1 - mid.astype(F32)).astype(BF16)
    return hi, mid, lo


def _dot(a, b):
    return jnp.dot(a, b, preferred_element_type=F32)


def _dot_f32_lhs(x, ones_rhs):
    hi, mid, lo = _split3(x)
    return _dot(hi, ones_rhs) + _dot(mid, ones_rhs) + _dot(lo, ones_rhs)


def _dot_f32_rhs(ones_lhs, x):
    hi, mid, lo = _split3(x)
    return _dot(ones_lhs, hi) + _dot(ones_lhs, mid) + _dot(ones_lhs, lo)


def _iota2(shape, axis):
    return lax.broadcasted_iota(jnp.int32, shape, axis)


def _proj_kernel(x_ref, w_ref, wg_ref, sc_ref, z_ref, g_ref):
    x = x_ref[...]
    z_ref[...] = (_dot(x, w_ref[...]) * sc_ref[...]).astype(z_ref.dtype)

    @pl.when(pl.program_id(1) == 0)
    def _():
        g_ref[...] = _dot(x, wg_ref[...])


def _in_proj(x, w, wg, col_scale):
    m, k = x.shape
    n = w.shape[1]
    tm, tn = PROJ_TM, PROJ_TN
    return pl.pallas_call(
        _proj_kernel,
        grid=(m // tm, n // tn),
        in_specs=[
            pl.BlockSpec((tm, k), lambda i, j: (i, 0)),
            pl.BlockSpec((k, tn), lambda i, j: (0, j)),
            pl.BlockSpec((k, GATE_PAD), lambda i, j: (0, 0)),
            pl.BlockSpec((1, tn), lambda i, j: (0, j)),
        ],
        out_specs=[
            pl.BlockSpec((tm, tn), lambda i, j: (i, j)),
            pl.BlockSpec((tm, GATE_PAD), lambda i, j: (i, 0)),
        ],
        out_shape=[
            jax.ShapeDtypeStruct((m, n), BF16),
            jax.ShapeDtypeStruct((m, GATE_PAD), F32),
        ],
        compiler_params=_params("parallel", "arbitrary"),
        name="in_proj",
    )(x, w, wg, col_scale)


def _attn_kernel(q_ref, k_ref, v_ref, f_ref, fb_ref, o_ref, cum_ref, m_ref, l_ref, acc_ref, *, tq):
    i = pl.program_id(2)
    rows_f = cum_ref.shape[0]
    groups = tq // LANES

    @pl.when(i == 0)
    def _():
        logf = _log_sigmoid(f_ref[0, 0] + fb_ref[0])
        upper = (_iota2((LANES, LANES), 0) <= _iota2((LANES, LANES), 1)).astype(BF16)
        within = _dot_f32_lhs(logf, upper)
        tot = jnp.broadcast_to(within[:, LANES - 1:LANES], (rows_f, LANES))
        below = (_iota2((rows_f, rows_f), 1) < _iota2((rows_f, rows_f), 0)).astype(BF16)
        cum_ref[...] = within + _dot_f32_rhs(below, tot)

    q = q_ref[0]
    m_ref[...] = jnp.full(m_ref.shape, -jnp.inf, F32)
    l_ref[...] = jnp.zeros(l_ref.shape, F32)
    acc_ref[...] = jnp.zeros(acc_ref.shape, F32)

    def tile(j, masked):
        start = pl.multiple_of(j * tq, tq)
        kj = k_ref[0, pl.ds(start, tq), :]
        vj = v_ref[0, pl.ds(start, tq), :]
        s = lax.dot_general(q, kj, (((1,), (1,)), ((), ())), preferred_element_type=F32)
        fk = jnp.concatenate(
            [cum_ref[pl.ds(j * groups + c, 1), :] for c in range(groups)], axis=1)
        s = s - fk
        if masked:
            s = jnp.where(_iota2((tq, tq), 1) <= _iota2((tq, tq), 0), s, -jnp.inf)
        m_old = m_ref[...]
        m_new = jnp.maximum(m_old, jnp.max(s, axis=1, keepdims=True))
        alpha = jnp.exp(m_old - m_new)
        p = jnp.exp(s - m_new)
        l_ref[...] = alpha * l_ref[...] + jnp.sum(p, axis=1, keepdims=True)
        acc_ref[...] = alpha * acc_ref[...] + _dot(p.astype(BF16), vj)
        m_ref[...] = m_new

    def body(j, carry):
        tile(j, False)
        return carry

    lax.fori_loop(0, i, body, 0)
    tile(i, True)
    o_ref[0] = (acc_ref[...] / l_ref[...]).astype(o_ref.dtype)


def _fox_attention(z3, f_rows, f_bias):
    b, s, _ = z3.shape
    tq = ATT_TQ
    return pl.pallas_call(
        functools.partial(_attn_kernel, tq=tq),
        grid=(b, HEADS, s // tq),
        in_specs=[
            pl.BlockSpec((1, tq, QK_DIM), lambda bi, h, i: (bi, i, h)),
            pl.BlockSpec((1, s, QK_DIM), lambda bi, h, i: (bi, 0, HEADS + h)),
            pl.BlockSpec((1, s, QK_DIM), lambda bi, h, i: (bi, 0, 2 * HEADS + h)),
            pl.BlockSpec((1, 1, s // LANES, LANES), lambda bi, h, i: (bi, h, 0, 0)),
            pl.BlockSpec((1, 1, LANES), lambda bi, h, i: (h, 0, 0)),
        ],
        out_specs=pl.BlockSpec((1, tq, QK_DIM), lambda bi, h, i: (bi, i, h)),
        out_shape=jax.ShapeDtypeStruct((b, s, ATT_WIDTH), BF16),
        scratch_shapes=[
            pltpu.VMEM((s // LANES, LANES), F32),
            pltpu.VMEM((tq, 1), F32),
            pltpu.VMEM((tq, 1), F32),
            pltpu.VMEM((tq, QK_DIM), F32),
        ],
        compiler_params=_params("parallel", "parallel", "arbitrary"),
        name="fox_attention",
    )(z3, z3, z3, f_rows, f_bias)


def _conv_kernel(a_ref, g_ref, kw_ref, kb_ref, ng_ref, nb_ref, o_ref, ybuf, *, ts):
    si = pl.program_id(1)

    @pl.when(si == 0)
    def _():
        ybuf[0:CONV_HALO, :] = jnp.zeros((CONV_HALO, CONV_CH), F32)

    @pl.when(si > 0)
    def _():
        ybuf[0:CONV_HALO, :] = ybuf[ts:ts + CONV_HALO, :]

    ybuf[CONV_HALO:CONV_HALO + ts, :] = a_ref[0].astype(F32) * _sigmoid(g_ref[0].astype(F32))

    first = CONV_HALO - (CONV_WIDTH - 1)
    acc = jnp.broadcast_to(kb_ref[...], (ts, CONV_CH))
    for w in range(CONV_WIDTH):
        acc = acc + ybuf[first + w:first + w + ts, :] * kw_ref[w:w + 1, :]
    y = _layer_norm(acc, ng_ref[...], nb_ref[...])
    o_ref[0] = (y * _sigmoid(y)).astype(o_ref.dtype)


def _conformer_conv(z3, kw, kb, ng, nb):
    b, s, _ = z3.shape
    ts = CONV_TS
    a_blk = 3 * ATT_WIDTH // CONV_CH
    vec = pl.BlockSpec((1, CONV_CH), lambda bi, si: (0, 0))
    return pl.pallas_call(
        functools.partial(_conv_kernel, ts=ts),
        grid=(b, s // ts),
        in_specs=[
            pl.BlockSpec((1, ts, CONV_CH), lambda bi, si: (bi, si, a_blk)),
            pl.BlockSpec((1, ts, CONV_CH), lambda bi, si: (bi, si, a_blk + 1)),
            pl.BlockSpec((CONV_WIDTH, CONV_CH), lambda bi, si: (0, 0)),
            vec, vec, vec,
        ],
        out_specs=pl.BlockSpec((1, ts, CONV_CH), lambda bi, si: (bi, si, 0)),
        out_shape=jax.ShapeDtypeStruct((b, s, CONV_CH), BF16),
        scratch_shapes=[pltpu.VMEM((ts + CONV_HALO, CONV_CH), F32)],
        compiler_params=_params("parallel", "arbitrary"),
        name="conformer_conv",
    )(z3, z3, kw, kb, ng, nb)


def _out_kernel(a1_ref, a2_ref, w1_ref, w2_ref, xr_ref, g_ref, b_ref, of_ref, ob_ref):
    y = _dot(a1_ref[...], w1_ref[...]) + _dot(a2_ref[...], w2_ref[...])
    x = _layer_norm(ALPHA * xr_ref[...] + y, g_ref[...], b_ref[...])
    of_ref[...] = x
    ob_ref[...] = x.astype(BF16)


def _out_proj(a1, a1_blk, a2, a2_blk, w, x_res, g, b):
    m, d = x_res.shape
    half = d // 2
    tm = OUT_TM
    vec = pl.BlockSpec((1, d), lambda i: (0, 0))
    return pl.pallas_call(
        _out_kernel,
        grid=(m // tm,),
        in_specs=[
            pl.BlockSpec((tm, half), lambda i: (i, a1_blk)),
            pl.BlockSpec((tm, half), lambda i: (i, a2_blk)),
            pl.BlockSpec((half, d), lambda i: (0, 0)),
            pl.BlockSpec((half, d), lambda i: (1, 0)),
            pl.BlockSpec((tm, d), lambda i: (i, 0)),
            vec, vec,
        ],
        out_specs=[pl.BlockSpec((tm, d), lambda i: (i, 0)), pl.BlockSpec((tm, d), lambda i: (i, 0))],
        out_shape=[jax.ShapeDtypeStruct((m, d), F32), jax.ShapeDtypeStruct((m, d), BF16)],
        compiler_params=_params("parallel"),
        name="out_proj_ln",
    )(a1, a2, w, w, x_res, g, b)


def _ffn_kernel(x_ref, xr_ref, wu_ref, wd_ref, g_ref, b_ref, o_ref):
    f = pl.program_id(1)
    h = jnp.maximum(_dot(x_ref[...], wu_ref[...]), 0.0)
    part = _dot((h * h).astype(BF16), wd_ref[...])

    @pl.when(f == 0)
    def _():
        o_ref[...] = part

    @pl.when(f > 0)
    def _():
        o_ref[...] += part

    @pl.when(f == pl.num_programs(1) - 1)
    def _():
        o_ref[...] = _layer_norm(ALPHA * xr_ref[...] + o_ref[...], g_ref[...], b_ref[...])


def _ffn(x_bf, x_res, w_up, w_down, g, b):
    m, d = x_res.shape
    dff = w_up.shape[1]
    tm, tf = FFN_TM, FFN_TF
    vec = pl.BlockSpec((1, d), lambda i, f: (0, 0))
    return pl.pallas_call(
        _ffn_kernel,
        grid=(m // tm, dff // tf),
        in_specs=[
            pl.BlockSpec((tm, d), lambda i, f: (i, 0)),
            pl.BlockSpec((tm, d), lambda i, f: (i, 0)),
            pl.BlockSpec((d, tf), lambda i, f: (0, f)),
            pl.BlockSpec((tf, d), lambda i, f: (f, 0)),
            vec, vec,
        ],
        out_specs=pl.BlockSpec((tm, d), lambda i, f: (i, 0)),
        out_shape=jax.ShapeDtypeStruct((m, d), F32),
        compiler_params=_params("parallel", "arbitrary"),
        name="ffn_ln",
    )(x_bf, x_res, w_up, w_down, g, b)


def _ple_kernel(x_ref, p_ref, wg_ref, wp_ref, of_ref, *maybe_ob_ref):
    x = x_ref[...]
    gate = _sigmoid(_dot(x.astype(BF16), wg_ref[...]))
    out = x + gate * _dot(p_ref[...].astype(BF16), wp_ref[...])
    of_ref[...] = out
    for ob_ref in maybe_ob_ref:
        ob_ref[...] = out.astype(BF16)


def _ple(x, p, wg, wp, with_bf16):
    m, d = x.shape
    tm = PLE_TM
    row = pl.BlockSpec((tm, d), lambda i: (i, 0))
    out_specs = [row]
    out_shape = [jax.ShapeDtypeStruct((m, d), F32)]
    if with_bf16:
        out_specs.append(row)
        out_shape.append(jax.ShapeDtypeStruct((m, d), BF16))
    return pl.pallas_call(
        _ple_kernel,
        grid=(m // tm,),
        in_specs=[
            row,
            pl.BlockSpec((tm, D_PLE), lambda i: (i, 0)),
            pl.BlockSpec((d, d), lambda i: (0, 0)),
            pl.BlockSpec((D_PLE, d), lambda i: (0, 0)),
        ],
        out_specs=out_specs,
        out_shape=out_shape,
        compiler_params=_params("parallel"),
        name="ple_gate",
    )(x, p, wg, wp)


def _mlstm_kernel(q_ref, k_ref, v_ref, o_ref, ig_ref, fg_ref, ib_ref, fb_ref, out_ref,
                  c_ref, n_ref, m_ref, b_ref, i_ref, *, chunk):
    L = chunk
    nc = b_ref.shape[0]
    logf = _log_sigmoid(fg_ref[0, 0] + fb_ref[0])
    upper = (_iota2((L, L), 0) <= _iota2((L, L), 1)).astype(BF16)
    b_ref[...] = _dot_f32_lhs(logf, upper)
    i_ref[...] = ig_ref[0, 0] + ib_ref[0]
    c_ref[...] = jnp.zeros(c_ref.shape, F32)
    n_ref[...] = jnp.zeros(n_ref.shape, F32)
    m_ref[...] = jnp.zeros(m_ref.shape, F32)

    def step(c, carry):
        r0 = pl.multiple_of(c * L, L)
        q = q_ref[0, pl.ds(r0, L), :]
        k = k_ref[0, pl.ds(r0, L), :]
        v = v_ref[0, pl.ds(r0, L), :]
        b_row = b_ref[pl.ds(c, 1), :]
        i_row = i_ref[pl.ds(c, 1), :]
        row_id = _iota2((L, L), 0)
        col_id = _iota2((L, L), 1)
        eye = row_id == col_id
        b_col = jnp.sum(jnp.where(eye, b_row, 0.0), axis=1, keepdims=True)
        i_col = jnp.sum(jnp.where(eye, i_row, 0.0), axis=1, keepdims=True)
        m_prev = m_ref[...]

        d_intra = jnp.where(col_id <= row_id, b_col - b_row + i_row, -jnp.inf)
        d_inter = b_col + m_prev
        m_t = jnp.maximum(d_inter, jnp.max(d_intra, axis=1, keepdims=True))
        w_intra = jnp.exp(d_intra - m_t)
        w_inter = jnp.exp(d_inter - m_t)
        s = lax.dot_general(q, k, (((1,), (1,)), ((), ())), preferred_element_type=F32) * w_intra
        c_old = c_ref[...]
        n_old = n_ref[...]
        num = w_inter * _dot(q, c_old.astype(BF16)) + _dot(s.astype(BF16), v)
        den = (w_inter * jnp.sum(q.astype(F32) * n_old, axis=1, keepdims=True)
               + jnp.sum(s, axis=1, keepdims=True))
        h = num / jnp.maximum(jnp.abs(den), jnp.exp(-m_t))
        gate = _sigmoid(o_ref[0, pl.ds(r0, L), :].astype(F32))
        out_ref[0, pl.ds(r0, L), :] = (gate * h).astype(out_ref.dtype)

        b_last = b_row[:, L - 1:L]
        m_new = jnp.maximum(b_last + m_prev, jnp.max(b_last - b_row + i_row, axis=1, keepdims=True))
        decay = jnp.exp(b_last + m_prev - m_new)
        wk = jnp.exp(b_last - b_col + i_col - m_new) * k.astype(F32)
        c_ref[...] = decay * c_old + lax.dot_general(
            wk.astype(BF16), v, (((0,), (0,)), ((), ())), preferred_element_type=F32)
        n_ref[...] = decay * n_old + jnp.sum(wk, axis=0, keepdims=True)
        m_ref[...] = m_new
        return carry

    lax.fori_loop(0, nc, step, 0)


def _mlstm(z3, i_rows, f_rows, i_bias, f_bias):
    b, s, _ = z3.shape
    L = ML_CHUNK
    nc = s // L
    v_blk0 = 2 * HEADS * QK_DIM // ML_V_DIM
    o_blk0 = v_blk0 + HEADS
    gate_spec = pl.BlockSpec((1, 1, nc, L), lambda bi, h: (bi, h, 0, 0))
    bias_spec = pl.BlockSpec((1, 1, L), lambda bi, h: (h, 0, 0))
    return pl.pallas_call(
        functools.partial(_mlstm_kernel, chunk=L),
        grid=(b, HEADS),
        in_specs=[
            pl.BlockSpec((1, s, QK_DIM), lambda bi, h: (bi, 0, h)),
            pl.BlockSpec((1, s, QK_DIM), lambda bi, h: (bi, 0, HEADS + h)),
            pl.BlockSpec((1, s, ML_V_DIM), lambda bi, h: (bi, 0, v_blk0 + h)),
            pl.BlockSpec((1, s, ML_V_DIM), lambda bi, h: (bi, 0, o_blk0 + h)),
            gate_spec, gate_spec, bias_spec, bias_spec,
        ],
        out_specs=pl.BlockSpec((1, s, ML_V_DIM), lambda bi, h: (bi, 0, h)),
        out_shape=jax.ShapeDtypeStruct((b, s, D_MODEL), BF16),
        scratch_shapes=[
            pltpu.VMEM((QK_DIM, ML_V_DIM), F32),
            pltpu.VMEM((1, QK_DIM), F32),
            pltpu.VMEM((1, 1), F32),
            pltpu.VMEM((nc, L), F32),
            pltpu.VMEM((nc, L), F32),
        ],
        compiler_params=_params("parallel", "parallel"),
        name="mlstm",
    )(z3, z3, z3, z3, i_rows, f_rows, i_bias, f_bias)


def _gate_rows(g, lo, seq, last):
    b = g.shape[0] // seq
    rows = g[:, lo:lo + HEADS].reshape(b, seq, HEADS).transpose(0, 2, 1)
    return rows.reshape(b, HEADS, seq // last, last)


def _lane_bias(bias, width):
    return jnp.broadcast_to(bias.astype(F32)[:, None, None], (HEADS, 1, width))


def _pad_gate_cols(w):
    return jnp.pad(w, ((0, 0), (0, GATE_PAD - w.shape[1]))).astype(BF16)


def _row(v):
    return v.astype(F32).reshape(1, -1)


def _ffn_ple(i, x_bf, x_f, p, ln_ffn_g, ln_ffn_b, w_up, w_down, w_ple, w_ple_gate, with_bf16):
    x2 = _ffn(x_bf, x_f, w_up[i].astype(BF16), w_down[i].astype(BF16), _row(ln_ffn_g[i]), _row(ln_ffn_b[i]))
    return _ple(x2, p[i].reshape(-1, D_PLE), w_ple_gate[i].astype(BF16), w_ple[i].astype(BF16), with_bf16)


def kernel(x, p, ev_w_in, ev_b_fgate, ev_dw_kernel, ev_dw_bias, ev_cnorm_g, ev_cnorm_b, ev_w_out,
           od_w_in, od_b_igate, od_b_fgate, od_w_out, ln_mix_g, ln_mix_b, w_up, w_down,
           ln_ffn_g, ln_ffn_b, w_ple, w_ple_gate):
    B, S, D = x.shape
    M = B * S
    x_f = x.reshape(M, D)
    x_bf = x_f.astype(BF16)

    w = ev_w_in[0]
    qkv_w = 3 * ATT_WIDTH
    w_main = jnp.concatenate([w[:, :qkv_w], w[:, qkv_w + HEADS:]], axis=1).astype(BF16)
    w_gate = _pad_gate_cols(w[:, qkv_w:qkv_w + HEADS])
    col_scale = jnp.concatenate(
        [jnp.full((1, ATT_WIDTH), QK_SCALE, F32), jnp.ones((1, w_main.shape[1] - ATT_WIDTH), F32)], axis=1)
    z, g = _in_proj(x_bf, w_main, w_gate, col_scale)
    z3 = z.reshape(B, S, -1)
    attn = _fox_attention(z3, _gate_rows(g, 0, S, LANES), _lane_bias(ev_b_fgate[0], LANES))
    conv = _conformer_conv(z3, ev_dw_kernel[0].astype(F32), _row(ev_dw_bias[0]),
                           _row(ev_cnorm_g[0]), _row(ev_cnorm_b[0]))
    x_f, x_bf = _out_proj(attn.reshape(M, -1), 0, conv.reshape(M, -1), 0, ev_w_out[0].astype(BF16),
                          x_f, _row(ln_mix_g[0]), _row(ln_mix_b[0]))
    x_f, x_bf = _ffn_ple(0, x_bf, x_f, p, ln_ffn_g, ln_ffn_b, w_up, w_down, w_ple, w_ple_gate, True)

    w = od_w_in[0]
    qkv_w = 2 * HEADS * QK_DIM + D_MODEL
    w_main = jnp.concatenate([w[:, :qkv_w], w[:, qkv_w + 2 * HEADS:]], axis=1).astype(BF16)
    w_gate = _pad_gate_cols(w[:, qkv_w:qkv_w + 2 * HEADS])
    col_scale = jnp.concatenate(
        [jnp.ones((1, ATT_WIDTH), F32), jnp.full((1, ATT_WIDTH), QK_SCALE, F32),
         jnp.ones((1, w_main.shape[1] - 2 * ATT_WIDTH), F32)], axis=1)
    z, g = _in_proj(x_bf, w_main, w_gate, col_scale)
    z3 = z.reshape(B, S, -1)
    hg = _mlstm(z3, _gate_rows(g, 0, S, ML_CHUNK), _gate_rows(g, HEADS, S, ML_CHUNK),
                _lane_bias(od_b_igate[0], ML_CHUNK), _lane_bias(od_b_fgate[0], ML_CHUNK))
    hg = hg.reshape(M, D)
    x_f, x_bf = _out_proj(hg, 0, hg, 1, od_w_out[0].astype(BF16), x_f, _row(ln_mix_g[1]), _row(ln_mix_b[1]))
    (x_f,) = _ffn_ple(1, x_bf, x_f, p, ln_ffn_g, ln_ffn_b, w_up, w_down, w_ple, w_ple_gate, False)
    return x_f.reshape(B, S, D)
```

```python
import functools

import jax
import jax.numpy as jnp
from jax import lax
from jax.experimental import pallas as pl
from jax.experimental.pallas import tpu as pltpu

F32 = jnp.float32
BF16 = jnp.bfloat16

D_MODEL = 2048
DEPTH = 2
D_PLE = 256
D_FF = 4 * D_MODEL
LN_EPS = 1e-5
HEADS = 8
QK_DIM = 128
ATT_WIDTH = HEADS * QK_DIM
CONV_CH = D_MODEL - ATT_WIDTH
CONV_WIDTH = 31
ML_V_DIM = D_MODEL // HEADS
ALPHA = (2 * DEPTH) ** 0.25
QK_SCALE = QK_DIM ** -0.5
LOG2E = 1.4426950408889634

LANES = 128
SUBLANES = 8
GATE_PAD = LANES
CONV_HALO = 32
VMEM_LIMIT = 58 * 1024 * 1024

PROJ_TM, PROJ_TN = 1024, 1024
ATT_TQ = 1024
ATT_QW = 256
ATT_TK = 512
CONV_TS = 512
CONV_ROWS = 32
OUT_TM = 512
FFN_TM, FFN_TF, FFN_ROWS = 1024, 512, 512
PLE_TM = 512
ML_CHUNK = 256
ML_TS = 1024


def _params(*sem):
    return pltpu.CompilerParams(dimension_semantics=sem, vmem_limit_bytes=VMEM_LIMIT)


def _layer_norm(t, g, b):
    mu = jnp.mean(t, axis=-1, keepdims=True)
    tc = t - mu
    var = jnp.mean(tc * tc, axis=-1, keepdims=True)
    return tc * lax.rsqrt(var + LN_EPS) * g + b


def _log_sigmoid(x):
    return -(jnp.maximum(-x, 0.0) + jnp.log1p(jnp.exp(-jnp.abs(x))))


def _sigmoid(x):
    return 1.0 / (1.0 + jnp.exp(-x))


def _split3(x):
    hi = x.astype(BF16)
    r1 = x - hi.astype(F32)
    mid = r1.astype(BF16)
    lo = (r1 - mid.astype(F32)).astype(BF16)
    return hi, mid, lo


def _dot(a, b):
    return jnp.dot(a, b, preferred_element_type=F32)


def _dot_f32_lhs(x, ones_rhs):
    hi, mid, lo = _split3(x)
    return _dot(hi, ones_rhs) + _dot(mid, ones_rhs) + _dot(lo, ones_rhs)


def _dot_f32_rhs(ones_lhs, x):
    hi, mid, lo = _split3(x)
    return _dot(ones_lhs, hi) + _dot(ones_lhs, mid) + _dot(ones_lhs, lo)


def _iota2(shape, axis):
    return lax.broadcasted_iota(jnp.int32, shape, axis)


def _proj_kernel(x_ref, w_ref, wg_ref, sc_ref, z_ref, g_ref):
    x = x_ref[...]
    z_ref[...] = (_dot(x, w_ref[...]) * sc_ref[...]).astype(z_ref.dtype)

    @pl.when(pl.program_id(1) == 0)
    def _():
        g_ref[...] = _dot(x, wg_ref[...])


def _in_proj(x, w, wg, col_scale):
    m, k = x.shape
    n = w.shape[1]
    tm, tn = PROJ_TM, PROJ_TN
    return pl.pallas_call(
        _proj_kernel,
        grid=(m // tm, n // tn),
        in_specs=[
            pl.BlockSpec((tm, k), lambda i, j: (i, 0)),
            pl.BlockSpec((k, tn), lambda i, j: (0, j)),
            pl.BlockSpec((k, GATE_PAD), lambda i, j: (0, 0)),
            pl.BlockSpec((1, tn), lambda i, j: (0, j)),
        ],
        out_specs=[
            pl.BlockSpec((tm, tn), lambda i, j: (i, j)),
            pl.BlockSpec((tm, GATE_PAD), lambda i, j: (i, 0)),
        ],
        out_shape=[
            jax.ShapeDtypeStruct((m, n), BF16),
            jax.ShapeDtypeStruct((m, GATE_PAD), F32),
        ],
        compiler_params=_params("parallel", "arbitrary"),
        name="in_proj",
    )(x, w, wg, col_scale)


def _attn_kernel(q_ref, k_ref, v_ref, ft_ref, fb_ref, o_ref, kaug_ref, vt_ref, s0_ref, s1_ref, *state, tq):
    s_refs = (s0_ref, s1_ref)
    i = pl.program_id(2)
    qw, tk = ATT_QW, ATT_TK
    nsub = tq // qw
    blocks_per_tile = tk // qw
    tiles_per_q = tq // tk
    assert tiles_per_q % 2 == 0
    m_refs, l_refs, acc_refs = state[:nsub], state[nsub:2 * nsub], state[2 * nsub:]
    n_grp = ft_ref.shape[3]

    @pl.when(i == 0)
    def _():
        logf = _log_sigmoid(ft_ref[0, 0] + fb_ref[0])
        lower = (_iota2((LANES, LANES), 1) <= _iota2((LANES, LANES), 0)).astype(BF16)
        within = _dot_f32_rhs(lower, logf)
        tot = jnp.broadcast_to(within[LANES - 1:LANES, :], (2 * SUBLANES, n_grp))
        before = (_iota2((n_grp, n_grp), 0) < _iota2((n_grp, n_grp), 1)).astype(BF16)
        cum = (within + _dot_f32_lhs(tot, before)[0:1, :]) * LOG2E
        lane = _iota2((LANES, LANES), 1)
        for r in range(n_grp):
            col = jnp.broadcast_to(cum[:, r:r + 1], (LANES, LANES))
            hi = col.astype(BF16).astype(F32)
            rest = col - hi
            mid = rest.astype(BF16).astype(F32)
            lo = rest - mid
            terms = jnp.where(lane == 0, hi, jnp.where(lane == 1, mid, jnp.where(lane == 2, lo, 0.0)))
            kaug_ref[r * LANES:(r + 1) * LANES, QK_DIM:] = terms.astype(BF16)
        kaug_ref[:, :QK_DIM] = k_ref[0]
        for blk in range(vt_ref.shape[0]):
            vt_ref[blk] = v_ref[0, blk * qw:(blk + 1) * qw, :].T

    q_aug = jnp.where(_iota2((qw, LANES), 1) < 3, -1.0, 0.0).astype(BF16)
    q_t = [jnp.concatenate([q_ref[0, u * qw:(u + 1) * qw, :], q_aug], axis=1).T for u in range(nsub)]
    for u in range(nsub):
        m_refs[u][...] = jnp.full((1, qw), -jnp.inf, F32)
        l_refs[u][...] = jnp.zeros((1, qw), F32)
        acc_refs[u][...] = jnp.zeros((QK_DIM, qw), F32)

    def scores(t, slot, subs):
        kk = kaug_ref[pl.ds(pl.multiple_of(t * tk, tk), tk), :]
        for u in subs:
            s_refs[slot][u] = _dot(kk, q_t[u])

    def update(t, slot, u, mask):
        s = s_refs[slot][u]
        if mask is not None:
            s = jnp.where(mask, s, -jnp.inf)
        m_old = m_refs[u][...]
        m_new = jnp.maximum(m_old, jnp.max(s, axis=0, keepdims=True))
        alpha = jnp.exp2(m_old - m_new)
        p = jnp.exp2(s - m_new)
        l_refs[u][...] = alpha * l_refs[u][...] + jnp.sum(p, axis=0, keepdims=True)
        pb = p.astype(BF16)
        pv = _dot(vt_ref[t * blocks_per_tile], pb[0:qw, :])
        for n in range(1, blocks_per_tile):
            pv = pv + _dot(vt_ref[t * blocks_per_tile + n], pb[n * qw:(n + 1) * qw, :])
        acc_refs[u][...] = alpha * acc_refs[u][...] + pv
        m_refs[u][...] = m_new

    n_main = i * tiles_per_q
    scores(0, 0, range(nsub))

    def body(jj, carry):
        for half in range(2):
            t = 2 * jj + half
            scores(t + 1, 1 - half, range(nsub))
            for u in range(nsub):
                update(t, half, u, None)
        return carry

    lax.fori_loop(0, n_main // 2, body, 0)
    for d in range(tiles_per_q):
        if d + 1 < tiles_per_q:
            scores(n_main + d + 1, (d + 1) % 2, [u for u in range(nsub) if (u + 1) * qw > (d + 1) * tk])
        for u in range(nsub):
            if (u + 1) * qw <= d * tk:
                continue
            mask = None
            if u * qw < (d + 1) * tk - 1:
                mask = d * tk + _iota2((tk, qw), 0) <= u * qw + _iota2((tk, qw), 1)
            update(n_main + d, d % 2, u, mask)
    for u in range(nsub):
        out = (acc_refs[u][...] / l_refs[u][...]).T
        o_ref[0, u * qw:(u + 1) * qw, :] = out.astype(o_ref.dtype)


def _fox_attention(z3, f_cols, f_bias):
    b, s, _ = z3.shape
    tq = ATT_TQ
    nsub = tq // ATT_QW
    return pl.pallas_call(
        functools.partial(_attn_kernel, tq=tq),
        grid=(b, HEADS, s // tq),
        in_specs=[
            pl.BlockSpec((1, tq, QK_DIM), lambda bi, h, i: (bi, i, h)),
            pl.BlockSpec((1, s, QK_DIM), lambda bi, h, i: (bi, 0, HEADS + h)),
            pl.BlockSpec((1, s, QK_DIM), lambda bi, h, i: (bi, 0, 2 * HEADS + h)),
            pl.BlockSpec((1, 1, LANES, s // LANES), lambda bi, h, i: (bi, h, 0, 0)),
            pl.BlockSpec((1, 1, s // LANES), lambda bi, h, i: (h, 0, 0)),
        ],
        out_specs=pl.BlockSpec((1, tq, QK_DIM), lambda bi, h, i: (bi, i, h)),
        out_shape=jax.ShapeDtypeStruct((b, s, ATT_WIDTH), BF16),
        scratch_shapes=(
            [pltpu.VMEM((s, 2 * QK_DIM), BF16),
             pltpu.VMEM((s // ATT_QW, QK_DIM, ATT_QW), BF16),
             pltpu.VMEM((nsub, ATT_TK, ATT_QW), F32),
             pltpu.VMEM((nsub, ATT_TK, ATT_QW), F32)]
            + [pltpu.VMEM((1, ATT_QW), F32)] * (2 * nsub)
            + [pltpu.VMEM((QK_DIM, ATT_QW), F32)] * nsub),
        compiler_params=_params("parallel", "parallel", "arbitrary"),
        name="fox_attention",
    )(z3, z3, z3, f_cols, f_bias)


def _conv_kernel(a_ref, g_ref, kw_ref, kb_ref, ng_ref, nb_ref, o_ref, ybuf, conv_ref, *, ts):
    si = pl.program_id(1)

    @pl.when(si == 0)
    def _():
        for r in range(SUBLANES):
            ybuf[r, 0:CONV_HALO - r, :] = jnp.zeros((CONV_HALO - r, CONV_CH), F32)

    @pl.when(si > 0)
    def _():
        for r in range(SUBLANES):
            ybuf[r, 0:CONV_HALO - r, :] = ybuf[r, ts:ts + CONV_HALO - r, :]

    y = a_ref[0].astype(F32) * _sigmoid(g_ref[0].astype(F32))
    for r in range(SUBLANES):
        ybuf[r, CONV_HALO - r:CONV_HALO - r + ts, :] = y

    first = CONV_HALO - (CONV_WIDTH - 1)
    rows = CONV_ROWS

    def block(rb, carry):
        r0 = pl.multiple_of(rb * rows, rows)
        acc = [kb_ref[...]] * (rows // SUBLANES)
        for w in range(CONV_WIDTH):
            q8, r = divmod(first + w, SUBLANES)
            tap = kw_ref[w]
            for sb in range(rows // SUBLANES):
                acc[sb] = acc[sb] + ybuf[r, pl.ds(r0 + (q8 + sb) * SUBLANES, SUBLANES), :] * tap
        conv_ref[pl.ds(r0, rows), :] = jnp.concatenate(acc, axis=0)
        return carry

    lax.fori_loop(0, ts // rows, block, 0)
    t = _layer_norm(conv_ref[...], ng_ref[...], nb_ref[...])
    o_ref[0] = (t * _sigmoid(t)).astype(o_ref.dtype)


def _conformer_conv(z3, kw, kb, ng, nb):
    b, s, _ = z3.shape
    ts = CONV_TS
    a_blk = 3 * ATT_WIDTH // CONV_CH
    vec = pl.BlockSpec((1, CONV_CH), lambda bi, si: (0, 0))
    return pl.pallas_call(
        functools.partial(_conv_kernel, ts=ts),
        grid=(b, s // ts),
        in_specs=[
            pl.BlockSpec((1, ts, CONV_CH), lambda bi, si: (bi, si, a_blk)),
            pl.BlockSpec((1, ts, CONV_CH), lambda bi, si: (bi, si, a_blk + 1)),
            pl.BlockSpec((CONV_WIDTH, SUBLANES, CONV_CH), lambda bi, si: (0, 0, 0)),
            pl.BlockSpec((SUBLANES, CONV_CH), lambda bi, si: (0, 0)),
            vec, vec,
        ],
        out_specs=pl.BlockSpec((1, ts, CONV_CH), lambda bi, si: (bi, si, 0)),
        out_shape=jax.ShapeDtypeStruct((b, s, CONV_CH), BF16),
        scratch_shapes=[pltpu.VMEM((SUBLANES, ts + CONV_HALO, CONV_CH), F32),
                        pltpu.VMEM((ts, CONV_CH), F32)],
        compiler_params=_params("parallel", "arbitrary"),
        name="conformer_conv",
    )(z3, z3, kw, kb, ng, nb)


def _out_kernel(a1_ref, a2_ref, w1_ref, w2_ref, xr_ref, g_ref, b_ref, of_ref, ob_ref):
    y = _dot(a1_ref[...], w1_ref[...]) + _dot(a2_ref[...], w2_ref[...])
    x = _layer_norm(ALPHA * xr_ref[...] + y, g_ref[...], b_ref[...])
    of_ref[...] = x
    ob_ref[...] = x.astype(BF16)


def _out_proj(a1, a1_blk, a2, a2_blk, w, x_res, g, b):
    m, d = x_res.shape
    half = d // 2
    tm = OUT_TM
    vec = pl.BlockSpec((1, d), lambda i: (0, 0))
    return pl.pallas_call(
        _out_kernel,
        grid=(m // tm,),
        in_specs=[
            pl.BlockSpec((tm, half), lambda i: (i, a1_blk)),
            pl.BlockSpec((tm, half), lambda i: (i, a2_blk)),
            pl.BlockSpec((half, d), lambda i: (0, 0)),
            pl.BlockSpec((half, d), lambda i: (1, 0)),
            pl.BlockSpec((tm, d), lambda i: (i, 0)),
            vec, vec,
        ],
        out_specs=[pl.BlockSpec((tm, d), lambda i: (i, 0)), pl.BlockSpec((tm, d), lambda i: (i, 0))],
        out_shape=[jax.ShapeDtypeStruct((m, d), F32), jax.ShapeDtypeStruct((m, d), BF16)],
        compiler_params=_params("parallel"),
        name="out_proj_ln",
    )(a1, a2, w, w, x_res, g, b)


def _ffn_kernel(x_ref, wu_ref, wd_ref, o_ref):
    @pl.when(pl.program_id(1) == 0)
    def _():
        o_ref[...] = jnp.zeros(o_ref.shape, F32)

    for c in range(o_ref.shape[0] // FFN_ROWS):
        rows = slice(c * FFN_ROWS, (c + 1) * FFN_ROWS)
        h = jnp.maximum(_dot(x_ref[rows, :], wu_ref[...]), 0.0)
        o_ref[rows, :] += _dot((h * h).astype(BF16), wd_ref[...])


def _ffn(x_bf, w_up, w_down):
    m, d = x_bf.shape
    dff = w_up.shape[1]
    tm, tf = FFN_TM, FFN_TF
    return pl.pallas_call(
        _ffn_kernel,
        grid=(m // tm, dff // tf),
        in_specs=[
            pl.BlockSpec((tm, d), lambda i, f: (i, 0)),
            pl.BlockSpec((d, tf), lambda i, f: (0, f)),
            pl.BlockSpec((tf, d), lambda i, f: (f, 0)),
        ],
        out_specs=pl.BlockSpec((tm, d), lambda i, f: (i, 0)),
        out_shape=jax.ShapeDtypeStruct((m, d), F32),
        compiler_params=_params("parallel", "arbitrary"),
        name="ffn",
    )(x_bf, w_up, w_down)


def _ple_kernel(xr_ref, y_ref, p_ref, g_ref, b_ref, wg_ref, wp_ref, of_ref, *maybe_ob_ref):
    x = _layer_norm(ALPHA * xr_ref[...] + y_ref[...], g_ref[...], b_ref[...])
    gate = _sigmoid(_dot(x.astype(BF16), wg_ref[...]))
    out = x + gate * _dot(p_ref[...].astype(BF16), wp_ref[...])
    of_ref[...] = out
    for ob_ref in maybe_ob_ref:
        ob_ref[...] = out.astype(BF16)


def _ln_ple(x_res, y, p, g, b, wg, wp, with_bf16):
    m, d = x_res.shape
    tm = PLE_TM
    row = pl.BlockSpec((tm, d), lambda i: (i, 0))
    vec = pl.BlockSpec((1, d), lambda i: (0, 0))
    once = pl.Buffered(1)
    out_specs = [row]
    out_shape = [jax.ShapeDtypeStruct((m, d), F32)]
    if with_bf16:
        out_specs.append(row)
        out_shape.append(jax.ShapeDtypeStruct((m, d), BF16))
    return pl.pallas_call(
        _ple_kernel,
        grid=(m // tm,),
        in_specs=[
            row, row,
            pl.BlockSpec((tm, D_PLE), lambda i: (i, 0)),
            vec, vec,
            pl.BlockSpec((d, d), lambda i: (0, 0), pipeline_mode=once),
            pl.BlockSpec((D_PLE, d), lambda i: (0, 0), pipeline_mode=once),
        ],
        out_specs=out_specs,
        out_shape=out_shape,
        compiler_params=_params("parallel"),
        name="ln_ple_gate",
    )(x_res, y, p, g, b, wg, wp)


def _mlstm_kernel(q_ref, k_ref, v_ref, o_ref, ig_ref, fg_ref, ib_ref, fb_ref, out_ref,
                  b_ref, r_ref, *state, chunk):
    L = chunk
    nck = q_ref.shape[1] // L
    c_refs, n_refs, m_refs = state[:HEADS], state[HEADS:2 * HEADS], state[2 * HEADS:]
    tn = (((0,), (0,)), ((), ()))
    nt = (((1,), (1,)), ((), ()))

    @pl.when(pl.program_id(1) == 0)
    def _():
        for h in range(HEADS):
            c_refs[h][...] = jnp.zeros((QK_DIM, ML_V_DIM), F32)
            n_refs[h][...] = jnp.zeros((1, QK_DIM), F32)
            m_refs[h][...] = jnp.zeros((1, 1), F32)

    upper = (_iota2((L, L), 0) <= _iota2((L, L), 1)).astype(BF16)
    for c in range(nck):
        logf = _log_sigmoid(fg_ref[0, c] + fb_ref[...])
        b_c = _dot_f32_lhs(jnp.concatenate([logf, logf], axis=0), upper)[0:HEADS]
        b_ref[c] = b_c
        r_ref[c] = ig_ref[0, c] + ib_ref[...] - b_c

    sub16 = _iota2((2 * SUBLANES, L), 0)
    sub_c = _iota2((2 * SUBLANES, 2 * LANES), 0)
    lane_c = _iota2((2 * SUBLANES, 2 * LANES), 1)
    rhs_const = jnp.where(((sub_c < 3) & (lane_c < LANES)) | ((sub_c >= 3) & (sub_c < 6) & (lane_c >= LANES)),
                          1.0, 0.0).astype(BF16)
    causal = _iota2((L, L), 1) <= _iota2((L, L), 0)

    def split_rows(x):
        hi = x.astype(BF16).astype(F32)
        rest = x - hi
        mid = rest.astype(BF16).astype(F32)
        return hi, mid, rest - mid

    def step(c, carry):
        r0 = pl.multiple_of(c * L, L)
        b_all = b_ref[c]
        r_all = r_ref[c]
        for h in range(HEADS):
            q = q_ref[0, pl.ds(r0, L), h * QK_DIM:(h + 1) * QK_DIM]
            k = k_ref[0, pl.ds(r0, L), h * QK_DIM:(h + 1) * QK_DIM]
            v = v_ref[0, pl.ds(r0, L), h * ML_V_DIM:(h + 1) * ML_V_DIM]
            b_row = b_all[h:h + 1, :]
            r_row = r_all[h:h + 1, :]
            bh, bm, bl = split_rows(b_row)
            rh, rm, rl = split_rows(r_row)
            lhs = jnp.where(sub16 == 0, bh, jnp.where(sub16 == 1, bm, jnp.where(sub16 == 2, bl,
                  jnp.where(sub16 == 3, rh, jnp.where(sub16 == 4, rm, jnp.where(sub16 == 5, rl,
                  jnp.where(sub16 < 9, 1.0, 0.0))))))).astype(BF16)
            rhs_d = jnp.where(sub16 < 3, 1.0, jnp.where(sub16 == 6, rh, jnp.where(sub16 == 7, rm,
                    jnp.where(sub16 == 8, rl, 0.0)))).astype(BF16)
            rhs = jnp.concatenate([rhs_d, rhs_const], axis=1)
            cols = lax.dot_general(lhs, rhs, tn, preferred_element_type=F32)
            b_col = cols[:, L:L + LANES]
            r_col = cols[:, L + LANES:]
            m_prev = m_refs[h][...]

            d_intra = jnp.where(causal, cols[:, 0:L], -jnp.inf)
            d_inter = b_col + m_prev
            m_t = jnp.maximum(d_inter, jnp.max(d_intra, axis=1, keepdims=True))
            w_intra = jnp.exp(d_intra - jnp.concatenate([m_t] * (L // LANES), axis=1))
            w_inter = jnp.exp(d_inter - m_t)
            s = lax.dot_general(q, k, nt, preferred_element_type=F32) * w_intra
            c_old = c_refs[h][...]
            n_old = n_refs[h][...]
            num = (jnp.concatenate([w_inter] * (ML_V_DIM // LANES), axis=1) * _dot(q, c_old.astype(BF16))
                   + _dot(s.astype(BF16), v))
            den = (w_inter[:, 0:1] * jnp.sum(q.astype(F32) * n_old, axis=1, keepdims=True)
                   + jnp.sum(s, axis=1, keepdims=True))
            hval = num / jnp.maximum(jnp.abs(den), jnp.exp(-m_t[:, 0:1]))
            gate = _sigmoid(o_ref[0, pl.ds(r0, L), h * ML_V_DIM:(h + 1) * ML_V_DIM].astype(F32))
            out_ref[0, pl.ds(r0, L), h * ML_V_DIM:(h + 1) * ML_V_DIM] = (gate * hval).astype(out_ref.dtype)

            b_last = b_row[:, L - 1:L]
            m_new = jnp.maximum(b_last + m_prev, jnp.max(b_last + r_row, axis=1, keepdims=True))
            decay = jnp.exp(b_last + m_prev - m_new)
            wk = jnp.exp(b_last + r_col - m_new) * k.astype(F32)
            c_refs[h][...] = decay * c_old + lax.dot_general(wk.astype(BF16), v, tn, preferred_element_type=F32)
            n_refs[h][...] = decay * n_old + jnp.sum(wk, axis=0, keepdims=True)
            m_refs[h][...] = m_new
        return carry

    lax.fori_loop(0, nck, step, 0)


def _mlstm(z3, i_rows, f_rows, i_bias, f_bias):
    b, s, _ = z3.shape
    L = ML_CHUNK
    ts = ML_TS
    nck = ts // L
    qk_w = HEADS * QK_DIM
    gate_spec = pl.BlockSpec((1, nck, HEADS, L), lambda bi, si: (bi, si, 0, 0))
    bias_spec = pl.BlockSpec((HEADS, L), lambda bi, si: (0, 0))
    return pl.pallas_call(
        functools.partial(_mlstm_kernel, chunk=L),
        grid=(b, s // ts),
        in_specs=[
            pl.BlockSpec((1, ts, qk_w), lambda bi, si: (bi, si, 0)),
            pl.BlockSpec((1, ts, qk_w), lambda bi, si: (bi, si, 1)),
            pl.BlockSpec((1, ts, D_MODEL), lambda bi, si: (bi, si, 1)),
            pl.BlockSpec((1, ts, D_MODEL), lambda bi, si: (bi, si, 2)),
            gate_spec, gate_spec, bias_spec, bias_spec,
        ],
        out_specs=pl.BlockSpec((1, ts, D_MODEL), lambda bi, si: (bi, si, 0)),
        out_shape=jax.ShapeDtypeStruct((b, s, D_MODEL), BF16),
        scratch_shapes=(
            [pltpu.VMEM((nck, HEADS, L), F32)] * 2
            + [pltpu.VMEM((QK_DIM, ML_V_DIM), F32)] * HEADS
            + [pltpu.VMEM((1, QK_DIM), F32)] * HEADS
            + [pltpu.VMEM((1, 1), F32)] * HEADS),
        compiler_params=_params("parallel", "arbitrary"),
        name="mlstm",
    )(z3, z3, z3, z3, i_rows, f_rows, i_bias, f_bias)


def _gate_rows(g, lo, seq, chunk):
    b = g.shape[0] // seq
    return g[:, lo:lo + HEADS].reshape(b, seq // chunk, chunk, HEADS).transpose(0, 1, 3, 2)


def _gate_cols(g, lo, seq):
    b = g.shape[0] // seq
    cols = g[:, lo:lo + HEADS].reshape(b, seq // LANES, LANES, HEADS)
    return cols.transpose(0, 3, 2, 1)


def _lane_bias(bias, width):
    return jnp.broadcast_to(bias.astype(F32)[:, None, None], (HEADS, 1, width))


def _pad_gate_cols(w):
    return jnp.pad(w, ((0, 0), (0, GATE_PAD - w.shape[1]))).astype(BF16)


def _row(v):
    return v.astype(F32).reshape(1, -1)


def _ffn_ple(i, x_bf, x_f, p, ln_ffn_g, ln_ffn_b, w_up, w_down, w_ple, w_ple_gate, with_bf16):
    y = _ffn(x_bf, w_up[i].astype(BF16), w_down[i].astype(BF16))
    return _ln_ple(x_f, y, p[i].reshape(-1, D_PLE), _row(ln_ffn_g[i]), _row(ln_ffn_b[i]),
                   w_ple_gate[i].astype(BF16), w_ple[i].astype(BF16), with_bf16)


def kernel(x, p, ev_w_in, ev_b_fgate, ev_dw_kernel, ev_dw_bias, ev_cnorm_g, ev_cnorm_b, ev_w_out,
           od_w_in, od_b_igate, od_b_fgate, od_w_out, ln_mix_g, ln_mix_b, w_up, w_down,
           ln_ffn_g, ln_ffn_b, w_ple, w_ple_gate):
    B, S, D = x.shape
    M = B * S
    x_f = x.reshape(M, D)
    x_bf = x_f.astype(BF16)

    w = ev_w_in[0]
    qkv_w = 3 * ATT_WIDTH
    w_main = jnp.concatenate([w[:, :qkv_w], w[:, qkv_w + HEADS:]], axis=1).astype(BF16)
    w_gate = _pad_gate_cols(w[:, qkv_w:qkv_w + HEADS])
    col_scale = jnp.concatenate(
        [jnp.full((1, ATT_WIDTH), QK_SCALE * LOG2E, F32), jnp.ones((1, w_main.shape[1] - ATT_WIDTH), F32)],
        axis=1)
    z, g = _in_proj(x_bf, w_main, w_gate, col_scale)
    z3 = z.reshape(B, S, -1)
    attn = _fox_attention(z3, _gate_cols(g, 0, S), _lane_bias(ev_b_fgate[0], S // LANES))
    taps = jnp.broadcast_to(ev_dw_kernel[0].astype(F32)[:, None, :], (CONV_WIDTH, SUBLANES, CONV_CH))
    conv = _conformer_conv(z3, taps, jnp.broadcast_to(_row(ev_dw_bias[0]), (SUBLANES, CONV_CH)),
                           _row(ev_cnorm_g[0]), _row(ev_cnorm_b[0]))
    x_f, x_bf = _out_proj(attn.reshape(M, -1), 0, conv.reshape(M, -1), 0, ev_w_out[0].astype(BF16),
                          x_f, _row(ln_mix_g[0]), _row(ln_mix_b[0]))
    x_f, x_bf = _ffn_ple(0, x_bf, x_f, p, ln_ffn_g, ln_ffn_b, w_up, w_down, w_ple, w_ple_gate, True)

    w = od_w_in[0]
    qkv_w = 2 * HEADS * QK_DIM + D_MODEL
    w_main = jnp.concatenate([w[:, :qkv_w], w[:, qkv_w + 2 * HEADS:]], axis=1).astype(BF16)
    w_gate = _pad_gate_cols(w[:, qkv_w:qkv_w + 2 * HEADS])
    col_scale = jnp.concatenate(
        [jnp.ones((1, ATT_WIDTH), F32), jnp.full((1, ATT_WIDTH), QK_SCALE, F32),
         jnp.ones((1, w_main.shape[1] - 2 * ATT_WIDTH), F32)], axis=1)
    z, g = _in_proj(x_bf, w_main, w_gate, col_scale)
    z3 = z.reshape(B, S, -1)
    hg = _mlstm(z3, _gate_rows(g, 0, S, ML_CHUNK), _gate_rows(g, HEADS, S, ML_CHUNK),
                _lane_bias(od_b_igate[0], ML_CHUNK)[:, 0], _lane_bias(od_b_fgate[0], ML_CHUNK)[:, 0])
    hg = hg.reshape(M, D)
    x_f, x_bf = _out_proj(hg, 0, hg, 1, od_w_out[0].astype(BF16), x_f, _row(ln_mix_g[1]), _row(ln_mix_b[1]))
    (x_f,) = _ffn_ple(1, x_bf, x_f, p, ln_ffn_g, ln_ffn_b, w_up, w_down, w_ple, w_ple_gate, False)
    return x_f.reshape(B, S, D)
```

```python
import functools

import jax
import jax.numpy as jnp
from jax import lax
from jax.experimental import pallas as pl
from jax.experimental.pallas import tpu as pltpu

F32 = jnp.float32
BF16 = jnp.bfloat16

D_MODEL = 2048
DEPTH = 2
D_PLE = 256
D_FF = 4 * D_MODEL
LN_EPS = 1e-5
HEADS = 8
QK_DIM = 128
ATT_WIDTH = HEADS * QK_DIM
CONV_CH = D_MODEL - ATT_WIDTH
CONV_WIDTH = 31
ML_V_DIM = D_MODEL // HEADS
ALPHA = (2 * DEPTH) ** 0.25
QK_SCALE = QK_DIM ** -0.5
LOG2E = 1.4426950408889634

LANES = 128
SUBLANES = 8
GATE_PAD = LANES
CONV_HALO = 32
VMEM_LIMIT = 58 * 1024 * 1024

PROJ_TM, PROJ_TN = 2048, 1024
ATT_TQ = 1024
ATT_QW = 256
ATT_TK = 512
CONV_TS = 512
CONV_ROWS = 32
OUT_TM = 512
FFN_TM, FFN_TF, FFN_ROWS = 1024, 1024, 512
PLE_TM = 512
OUT_SPLIT = 4
PLE_SPLIT = 2
ML_CHUNK = 256
ML_TS = 1024


def _params(*sem):
    return pltpu.CompilerParams(dimension_semantics=sem, vmem_limit_bytes=VMEM_LIMIT)


def _layer_norm(t, g, b):
    mu = jnp.mean(t, axis=-1, keepdims=True)
    tc = t - mu
    var = jnp.mean(tc * tc, axis=-1, keepdims=True)
    return tc * lax.rsqrt(var + LN_EPS) * g + b


def _log_sigmoid(x):
    return -(jnp.maximum(-x, 0.0) + jnp.log1p(jnp.exp(-jnp.abs(x))))


def _sigmoid(x):
    return 1.0 / (1.0 + jnp.exp(-x))


def _split3(x):
    hi = x.astype(BF16)
    r1 = x - hi.astype(F32)
    mid = r1.astype(BF16)
    lo = (r1 - mid.astype(F32)).astype(BF16)
    return hi, mid, lo


def _dot(a, b):
    return jnp.dot(a, b, preferred_element_type=F32)


def _dot_f32_lhs(x, ones_rhs):
    hi, mid, lo = _split3(x)
    return _dot(hi, ones_rhs) + _dot(mid, ones_rhs) + _dot(lo, ones_rhs)


def _dot_f32_rhs(ones_lhs, x):
    hi, mid, lo = _split3(x)
    return _dot(ones_lhs, hi) + _dot(ones_lhs, mid) + _dot(ones_lhs, lo)


def _iota2(shape, axis):
    return lax.broadcasted_iota(jnp.int32, shape, axis)


def _proj_kernel(x_ref, w_ref, wgt_ref, sc_ref, z_ref, gt_ref, *maybe_xb_ref):
    first = pl.program_id(1) == 0
    if maybe_xb_ref:
        (xb_ref,) = maybe_xb_ref

        @pl.when(first)
        def _():
            xb_ref[...] = x_ref[...].astype(BF16)
    else:
        xb_ref = x_ref
    z_ref[...] = (_dot(xb_ref[...], w_ref[...]) * sc_ref[...]).astype(z_ref.dtype)

    @pl.when(first)
    def _():
        gt_ref[...] = lax.dot_general(wgt_ref[...], xb_ref[...], (((1,), (1,)), ((), ())),
                                      preferred_element_type=F32)


def _in_proj(x, w, wg_t, col_scale):
    m, k = x.shape
    n = w.shape[1]
    tn = PROJ_TN
    tm = PROJ_TM if x.dtype == BF16 else PROJ_TM // 2
    scratch = [] if x.dtype == BF16 else [pltpu.VMEM((tm, k), BF16)]
    return pl.pallas_call(
        _proj_kernel,
        grid=(m // tm, n // tn),
        in_specs=[
            pl.BlockSpec((tm, k), lambda i, j: (i, 0)),
            pl.BlockSpec((k, tn), lambda i, j: (0, j)),
            pl.BlockSpec((GATE_PAD, k), lambda i, j: (0, 0)),
            pl.BlockSpec((1, tn), lambda i, j: (0, j)),
        ],
        out_specs=[
            pl.BlockSpec((tm, tn), lambda i, j: (i, j)),
            pl.BlockSpec((GATE_PAD, tm), lambda i, j: (0, i)),
        ],
        out_shape=[
            jax.ShapeDtypeStruct((m, n), BF16),
            jax.ShapeDtypeStruct((GATE_PAD, m), F32),
        ],
        scratch_shapes=scratch,
        compiler_params=_params("parallel", "arbitrary"),
        name="in_proj",
    )(x, w, wg_t, col_scale)


def _attn_kernel(q_ref, k_ref, v_ref, f_ref, fb_ref, o_ref, kaug_ref, vt_ref, s0_ref, s1_ref, *state, tq):
    s_refs = (s0_ref, s1_ref)
    i = pl.program_id(2)
    qw, tk = ATT_QW, ATT_TK
    nsub = tq // qw
    blocks_per_tile = tk // qw
    tiles_per_q = tq // tk
    assert tiles_per_q % 2 == 0
    m_refs, l_refs, acc_refs = state[:nsub], state[nsub:2 * nsub], state[2 * nsub:]
    n_grp = f_ref.shape[2]

    @pl.when(i == 0)
    def _():
        f_rows = jnp.concatenate(
            [f_ref[0, 0] + fb_ref[0], jnp.zeros((LANES - n_grp, LANES), F32)], axis=0)
        ft = f_rows.T[:, 0:n_grp]
        logf = _log_sigmoid(ft)
        lower = (_iota2((LANES, LANES), 1) <= _iota2((LANES, LANES), 0)).astype(BF16)
        within = _dot_f32_rhs(lower, logf)
        tot = jnp.broadcast_to(within[LANES - 1:LANES, :], (2 * SUBLANES, n_grp))
        before = (_iota2((n_grp, n_grp), 0) < _iota2((n_grp, n_grp), 1)).astype(BF16)
        cum = (within + _dot_f32_lhs(tot, before)[0:1, :]) * LOG2E
        lane = _iota2((LANES, LANES), 1)
        for r in range(n_grp):
            col = jnp.broadcast_to(cum[:, r:r + 1], (LANES, LANES))
            hi = col.astype(BF16).astype(F32)
            rest = col - hi
            mid = rest.astype(BF16).astype(F32)
            lo = rest - mid
            terms = jnp.where(lane == 0, hi, jnp.where(lane == 1, mid, jnp.where(lane == 2, lo, 0.0)))
            kaug_ref[r * LANES:(r + 1) * LANES, QK_DIM:] = terms.astype(BF16)
        kaug_ref[:, :QK_DIM] = k_ref[0]
        for blk in range(vt_ref.shape[0]):
            vt_ref[blk] = v_ref[0, blk * qw:(blk + 1) * qw, :].T

    q_aug = jnp.where(_iota2((qw, LANES), 1) < 3, -1.0, 0.0).astype(BF16)
    q_t = [jnp.concatenate([q_ref[0, u * qw:(u + 1) * qw, :], q_aug], axis=1).T for u in range(nsub)]
    for u in range(nsub):
        m_refs[u][...] = jnp.full((1, qw), -jnp.inf, F32)
        l_refs[u][...] = jnp.zeros((1, qw), F32)
        acc_refs[u][...] = jnp.zeros((QK_DIM, qw), F32)

    def scores(t, slot, subs):
        kk = kaug_ref[pl.ds(pl.multiple_of(t * tk, tk), tk), :]
        for u in subs:
            s_refs[slot][u] = _dot(kk, q_t[u])

    def update(t, slot, u, mask):
        s = s_refs[slot][u]
        if mask is not None:
            s = jnp.where(mask, s, -jnp.inf)
        m_old = m_refs[u][...]
        m_new = jnp.maximum(m_old, jnp.max(s, axis=0, keepdims=True))
        alpha = jnp.exp2(m_old - m_new)
        p = jnp.exp2(s - m_new)
        l_refs[u][...] = alpha * l_refs[u][...] + jnp.sum(p, axis=0, keepdims=True)
        pb = p.astype(BF16)
        pv = _dot(vt_ref[t * blocks_per_tile], pb[0:qw, :])
        for n in range(1, blocks_per_tile):
            pv = pv + _dot(vt_ref[t * blocks_per_tile + n], pb[n * qw:(n + 1) * qw, :])
        acc_refs[u][...] = alpha * acc_refs[u][...] + pv
        m_refs[u][...] = m_new

    n_main = i * tiles_per_q
    scores(0, 0, range(nsub))

    def body(jj, carry):
        for half in range(2):
            t = 2 * jj + half
            scores(t + 1, 1 - half, range(nsub))
            for u in range(nsub):
                update(t, half, u, None)
        return carry

    lax.fori_loop(0, n_main // 2, body, 0)
    for d in range(tiles_per_q):
        if d + 1 < tiles_per_q:
            scores(n_main + d + 1, (d + 1) % 2, [u for u in range(nsub) if (u + 1) * qw > (d + 1) * tk])
        for u in range(nsub):
            if (u + 1) * qw <= d * tk:
                continue
            mask = None
            if u * qw < (d + 1) * tk - 1:
                mask = d * tk + _iota2((tk, qw), 0) <= u * qw + _iota2((tk, qw), 1)
            update(n_main + d, d % 2, u, mask)
    for u in range(nsub):
        out = (acc_refs[u][...] / l_refs[u][...]).T
        o_ref[0, u * qw:(u + 1) * qw, :] = out.astype(o_ref.dtype)


def _fox_attention(z3, f_rows, f_bias):
    b, s, _ = z3.shape
    tq = ATT_TQ
    nsub = tq // ATT_QW
    return pl.pallas_call(
        functools.partial(_attn_kernel, tq=tq),
        grid=(b, HEADS, s // tq),
        in_specs=[
            pl.BlockSpec((1, tq, QK_DIM), lambda bi, h, i: (bi, i, h)),
            pl.BlockSpec((1, s, QK_DIM), lambda bi, h, i: (bi, 0, HEADS + h)),
            pl.BlockSpec((1, s, QK_DIM), lambda bi, h, i: (bi, 0, 2 * HEADS + h)),
            pl.BlockSpec((1, 1, s // LANES, LANES), lambda bi, h, i: (h, bi, 0, 0)),
            pl.BlockSpec((1, 1, LANES), lambda bi, h, i: (h, 0, 0)),
        ],
        out_specs=pl.BlockSpec((1, tq, QK_DIM), lambda bi, h, i: (bi, i, h)),
        out_shape=jax.ShapeDtypeStruct((b, s, ATT_WIDTH), BF16),
        scratch_shapes=(
            [pltpu.VMEM((s, 2 * QK_DIM), BF16),
             pltpu.VMEM((s // ATT_QW, QK_DIM, ATT_QW), BF16),
             pltpu.VMEM((nsub, ATT_TK, ATT_QW), F32),
             pltpu.VMEM((nsub, ATT_TK, ATT_QW), F32)]
            + [pltpu.VMEM((1, ATT_QW), F32)] * (2 * nsub)
            + [pltpu.VMEM((QK_DIM, ATT_QW), F32)] * nsub),
        compiler_params=_params("parallel", "parallel", "arbitrary"),
        name="fox_attention",
    )(z3, z3, z3, f_rows, f_bias)


def _conv_kernel(a_ref, g_ref, kw_ref, kb_ref, ng_ref, nb_ref, o_ref, ybuf, conv_ref, *, ts):
    si = pl.program_id(1)

    @pl.when(si == 0)
    def _():
        for r in range(SUBLANES):
            ybuf[r, 0:CONV_HALO - r, :] = jnp.zeros((CONV_HALO - r, CONV_CH), F32)

    @pl.when(si > 0)
    def _():
        for r in range(SUBLANES):
            ybuf[r, 0:CONV_HALO - r, :] = ybuf[r, ts:ts + CONV_HALO - r, :]

    y = a_ref[0].astype(F32) * _sigmoid(g_ref[0].astype(F32))
    for r in range(SUBLANES):
        ybuf[r, CONV_HALO - r:CONV_HALO - r + ts, :] = y

    first = CONV_HALO - (CONV_WIDTH - 1)
    rows = CONV_ROWS

    def block(rb, carry):
        r0 = pl.multiple_of(rb * rows, rows)
        acc = [kb_ref[...]] * (rows // SUBLANES)
        for w in range(CONV_WIDTH):
            q8, r = divmod(first + w, SUBLANES)
            tap = kw_ref[w]
            for sb in range(rows // SUBLANES):
                acc[sb] = acc[sb] + ybuf[r, pl.ds(r0 + (q8 + sb) * SUBLANES, SUBLANES), :] * tap
        conv_ref[pl.ds(r0, rows), :] = jnp.concatenate(acc, axis=0)
        return carry

    lax.fori_loop(0, ts // rows, block, 0)
    t = _layer_norm(conv_ref[...], ng_ref[...], nb_ref[...])
    o_ref[0] = (t * _sigmoid(t)).astype(o_ref.dtype)


def _conformer_conv(z3, kw, kb, ng, nb):
    b, s, _ = z3.shape
    ts = CONV_TS
    a_blk = 3 * ATT_WIDTH // CONV_CH
    vec = pl.BlockSpec((1, CONV_CH), lambda bi, si: (0, 0))
    return pl.pallas_call(
        functools.partial(_conv_kernel, ts=ts),
        grid=(b, s // ts),
        in_specs=[
            pl.BlockSpec((1, ts, CONV_CH), lambda bi, si: (bi, si, a_blk)),
            pl.BlockSpec((1, ts, CONV_CH), lambda bi, si: (bi, si, a_blk + 1)),
            pl.BlockSpec((CONV_WIDTH, SUBLANES, CONV_CH), lambda bi, si: (0, 0, 0)),
            pl.BlockSpec((SUBLANES, CONV_CH), lambda bi, si: (0, 0)),
            vec, vec,
        ],
        out_specs=pl.BlockSpec((1, ts, CONV_CH), lambda bi, si: (bi, si, 0)),
        out_shape=jax.ShapeDtypeStruct((b, s, CONV_CH), BF16),
        scratch_shapes=[pltpu.VMEM((SUBLANES, ts + CONV_HALO, CONV_CH), F32),
                        pltpu.VMEM((ts, CONV_CH), F32)],
        compiler_params=_params("parallel", "arbitrary"),
        name="conformer_conv",
    )(z3, z3, kw, kb, ng, nb)


def _out_kernel(a1_ref, a2_ref, w1_ref, w2_ref, xr_ref, g_ref, b_ref, of_ref, ob_ref):
    tm = of_ref.shape[0]
    for c in range(OUT_SPLIT):
        rows = slice(c * tm // OUT_SPLIT, (c + 1) * tm // OUT_SPLIT)
        y = _dot(a1_ref[rows, :], w1_ref[...]) + _dot(a2_ref[rows, :], w2_ref[...])
        x = _layer_norm(ALPHA * xr_ref[rows, :] + y, g_ref[...], b_ref[...])
        of_ref[rows, :] = x
        ob_ref[rows, :] = x.astype(BF16)


def _out_proj(a1, a1_blk, a2, a2_blk, w, x_res, g, b):
    m, d = x_res.shape
    half = d // 2
    tm = OUT_TM
    vec = pl.BlockSpec((1, d), lambda i: (0, 0))
    return pl.pallas_call(
        _out_kernel,
        grid=(m // tm,),
        in_specs=[
            pl.BlockSpec((tm, half), lambda i: (i, a1_blk)),
            pl.BlockSpec((tm, half), lambda i: (i, a2_blk)),
            pl.BlockSpec((half, d), lambda i: (0, 0)),
            pl.BlockSpec((half, d), lambda i: (1, 0)),
            pl.BlockSpec((tm, d), lambda i: (i, 0)),
            vec, vec,
        ],
        out_specs=[pl.BlockSpec((tm, d), lambda i: (i, 0)), pl.BlockSpec((tm, d), lambda i: (i, 0))],
        out_shape=[jax.ShapeDtypeStruct((m, d), F32), jax.ShapeDtypeStruct((m, d), BF16)],
        compiler_params=_params("parallel"),
        name="out_proj_ln",
    )(a1, a2, w, w, x_res, g, b)


def _ffn_kernel(x_ref, wu_ref, wd_ref, o_ref):
    @pl.when(pl.program_id(1) == 0)
    def _():
        o_ref[...] = jnp.zeros(o_ref.shape, F32)

    for c in range(o_ref.shape[0] // FFN_ROWS):
        rows = slice(c * FFN_ROWS, (c + 1) * FFN_ROWS)
        h = jnp.maximum(_dot(x_ref[rows, :], wu_ref[...]), 0.0)
        o_ref[rows, :] += _dot((h * h).astype(BF16), wd_ref[...])


def _ffn(x_bf, w_up, w_down, layer):
    m, d = x_bf.shape
    dff = w_up.shape[2]
    tm, tf = FFN_TM, FFN_TF
    return pl.pallas_call(
        _ffn_kernel,
        grid=(m // tm, dff // tf),
        in_specs=[
            pl.BlockSpec((tm, d), lambda i, f: (i, 0)),
            pl.BlockSpec((None, d, tf), lambda i, f: (layer, 0, f)),
            pl.BlockSpec((None, tf, d), lambda i, f: (layer, f, 0)),
        ],
        out_specs=pl.BlockSpec((tm, d), lambda i, f: (i, 0)),
        out_shape=jax.ShapeDtypeStruct((m, d), F32),
        compiler_params=_params("parallel", "arbitrary"),
        name="ffn",
    )(x_bf, w_up, w_down)


def _ple_kernel(xr_ref, y_ref, p_ref, g_ref, b_ref, wg_ref, wp_ref, of_ref, *maybe_ob_ref):
    tm = of_ref.shape[0]
    for c in range(PLE_SPLIT):
        rows = slice(c * tm // PLE_SPLIT, (c + 1) * tm // PLE_SPLIT)
        x = _layer_norm(ALPHA * xr_ref[rows, :] + y_ref[rows, :], g_ref[...], b_ref[...])
        gate = _sigmoid(_dot(x.astype(BF16), wg_ref[...]))
        out = x + gate * _dot(p_ref[rows, :].astype(BF16), wp_ref[...])
        of_ref[rows, :] = out
        for ob_ref in maybe_ob_ref:
            ob_ref[rows, :] = out.astype(BF16)


def _ln_ple(x_res, y, p, g, b, wg, wp, layer, with_bf16):
    m, d = x_res.shape
    tm = PLE_TM
    row = pl.BlockSpec((tm, d), lambda i: (i, 0))
    vec = pl.BlockSpec((1, d), lambda i: (0, 0))
    once = pl.Buffered(1)
    out_specs = [row]
    out_shape = [jax.ShapeDtypeStruct((m, d), F32)]
    if with_bf16:
        out_specs.append(row)
        out_shape.append(jax.ShapeDtypeStruct((m, d), BF16))
    return pl.pallas_call(
        _ple_kernel,
        grid=(m // tm,),
        in_specs=[
            row, row,
            pl.BlockSpec((None, tm, D_PLE), lambda i: (layer, i, 0)),
            vec, vec,
            pl.BlockSpec((None, d, d), lambda i: (layer, 0, 0), pipeline_mode=once),
            pl.BlockSpec((None, D_PLE, d), lambda i: (layer, 0, 0), pipeline_mode=once),
        ],
        out_specs=out_specs,
        out_shape=out_shape,
        compiler_params=_params("parallel"),
        name="ln_ple_gate",
    )(x_res, y, p, g, b, wg, wp)


def _mlstm_kernel(q_ref, k_ref, v_ref, o_ref, g_ref, ib_ref, fb_ref, out_ref,
                  b_ref, r_ref, *state, chunk):
    L = chunk
    nck = q_ref.shape[1] // L
    c_refs, n_refs, m_refs = state[:HEADS], state[HEADS:2 * HEADS], state[2 * HEADS:]
    tn = (((0,), (0,)), ((), ()))
    nt = (((1,), (1,)), ((), ()))

    @pl.when(pl.program_id(1) == 0)
    def _():
        for h in range(HEADS):
            c_refs[h][...] = jnp.zeros((QK_DIM, ML_V_DIM), F32)
            n_refs[h][...] = jnp.zeros((1, QK_DIM), F32)
            m_refs[h][...] = jnp.zeros((1, 1), F32)

    upper = (_iota2((L, L), 0) <= _iota2((L, L), 1)).astype(BF16)
    for c in range(nck):
        gates = g_ref[0, c]
        logf = _log_sigmoid(gates[HEADS:] + fb_ref[...])
        b_c = _dot_f32_lhs(jnp.concatenate([logf, logf], axis=0), upper)[0:HEADS]
        b_ref[c] = b_c
        r_ref[c] = gates[:HEADS] + ib_ref[...] - b_c

    sub16 = _iota2((2 * SUBLANES, L), 0)
    sub_c = _iota2((2 * SUBLANES, 2 * LANES), 0)
    lane_c = _iota2((2 * SUBLANES, 2 * LANES), 1)
    rhs_const = jnp.where(((sub_c < 3) & (lane_c < LANES)) | ((sub_c >= 3) & (sub_c < 6) & (lane_c >= LANES)),
                          1.0, 0.0).astype(BF16)
    causal = _iota2((L, L), 1) <= _iota2((L, L), 0)

    def split_rows(x):
        hi = x.astype(BF16).astype(F32)
        rest = x - hi
        mid = rest.astype(BF16).astype(F32)
        return hi, mid, rest - mid

    def step(c, carry):
        r0 = pl.multiple_of(c * L, L)
        b_all = b_ref[c]
        r_all = r_ref[c]
        for h in range(HEADS):
            q = q_ref[0, pl.ds(r0, L), h * QK_DIM:(h + 1) * QK_DIM]
            k = k_ref[0, pl.ds(r0, L), h * QK_DIM:(h + 1) * QK_DIM]
            v = v_ref[0, pl.ds(r0, L), h * ML_V_DIM:(h + 1) * ML_V_DIM]
            b_row = b_all[h:h + 1, :]
            r_row = r_all[h:h + 1, :]
            bh, bm, bl = split_rows(b_row)
            rh, rm, rl = split_rows(r_row)
            lhs = jnp.where(sub16 == 0, bh, jnp.where(sub16 == 1, bm, jnp.where(sub16 == 2, bl,
                  jnp.where(sub16 == 3, rh, jnp.where(sub16 == 4, rm, jnp.where(sub16 == 5, rl,
                  jnp.where(sub16 < 9, 1.0, 0.0))))))).astype(BF16)
            rhs_d = jnp.where(sub16 < 3, 1.0, jnp.where(sub16 == 6, rh, jnp.where(sub16 == 7, rm,
                    jnp.where(sub16 == 8, rl, 0.0)))).astype(BF16)
            rhs = jnp.concatenate([rhs_d, rhs_const], axis=1)
            cols = lax.dot_general(lhs, rhs, tn, preferred_element_type=F32)
            b_col = cols[:, L:L + LANES]
            r_col = cols[:, L + LANES:]
            m_prev = m_refs[h][...]

            d_intra = jnp.where(causal, cols[:, 0:L], -jnp.inf)
            d_inter = b_col + m_prev
            m_t = jnp.maximum(d_inter, jnp.max(d_intra, axis=1, keepdims=True))
            w_intra = jnp.exp(d_intra - jnp.concatenate([m_t] * (L // LANES), axis=1))
            w_inter = jnp.exp(d_inter - m_t)
            s = lax.dot_general(q, k, nt, preferred_element_type=F32) * w_intra
            c_old = c_refs[h][...]
            n_old = n_refs[h][...]
            num = (jnp.concatenate([w_inter] * (ML_V_DIM // LANES), axis=1) * _dot(q, c_old.astype(BF16))
                   + _dot(s.astype(BF16), v))
            den = (w_inter[:, 0:1] * jnp.sum(q.astype(F32) * n_old, axis=1, keepdims=True)
                   + jnp.sum(s, axis=1, keepdims=True))
            hval = num / jnp.maximum(jnp.abs(den), jnp.exp(-m_t[:, 0:1]))
            gate = _sigmoid(o_ref[0, pl.ds(r0, L), h * ML_V_DIM:(h + 1) * ML_V_DIM].astype(F32))
            out_ref[0, pl.ds(r0, L), h * ML_V_DIM:(h + 1) * ML_V_DIM] = (gate * hval).astype(out_ref.dtype)

            b_last = b_row[:, L - 1:L]
            m_new = jnp.maximum(b_last + m_prev, jnp.max(b_last + r_row, axis=1, keepdims=True))
            decay = jnp.exp(b_last + m_prev - m_new)
            wk = jnp.exp(b_last + r_col - m_new) * k.astype(F32)
            c_refs[h][...] = decay * c_old + lax.dot_general(wk.astype(BF16), v, tn, preferred_element_type=F32)
            n_refs[h][...] = decay * n_old + jnp.sum(wk, axis=0, keepdims=True)
            m_refs[h][...] = m_new
        return carry

    lax.fori_loop(0, nck, step, 0)


def _mlstm(z3, gate_rows, i_bias, f_bias):
    b, s, _ = z3.shape
    L = ML_CHUNK
    ts = ML_TS
    nck = ts // L
    qk_w = HEADS * QK_DIM
    gate_spec = pl.BlockSpec((1, nck, 2 * HEADS, L), lambda bi, si: (bi, si, 0, 0))
    bias_spec = pl.BlockSpec((HEADS, L), lambda bi, si: (0, 0))
    return pl.pallas_call(
        functools.partial(_mlstm_kernel, chunk=L),
        grid=(b, s // ts),
        in_specs=[
            pl.BlockSpec((1, ts, qk_w), lambda bi, si: (bi, si, 0)),
            pl.BlockSpec((1, ts, qk_w), lambda bi, si: (bi, si, 1)),
            pl.BlockSpec((1, ts, D_MODEL), lambda bi, si: (bi, si, 1)),
            pl.BlockSpec((1, ts, D_MODEL), lambda bi, si: (bi, si, 2)),
            gate_spec, bias_spec, bias_spec,
        ],
        out_specs=pl.BlockSpec((1, ts, D_MODEL), lambda bi, si: (bi, si, 0)),
        out_shape=jax.ShapeDtypeStruct((b, s, D_MODEL), BF16),
        scratch_shapes=(
            [pltpu.VMEM((nck, HEADS, L), F32)] * 2
            + [pltpu.VMEM((QK_DIM, ML_V_DIM), F32)] * HEADS
            + [pltpu.VMEM((1, QK_DIM), F32)] * HEADS
            + [pltpu.VMEM((1, 1), F32)] * HEADS),
        compiler_params=_params("parallel", "arbitrary"),
        name="mlstm",
    )(z3, z3, z3, z3, gate_rows, i_bias, f_bias)


def _split_w_in(w, gate_lo, n_gate):
    wb = w.astype(BF16)
    w_main = jnp.concatenate([wb[:, :gate_lo], wb[:, gate_lo + n_gate:]], axis=1)
    w_gate_t = jnp.pad(wb[:, gate_lo:gate_lo + n_gate].T, ((0, GATE_PAD - n_gate), (0, 0)))
    return w_main, w_gate_t


def _col_scale(n, lo, hi, value):
    col = jnp.arange(n)[None, :]
    return jnp.where((col >= lo) & (col < hi), value, 1.0).astype(F32)


def _row(v):
    return v.astype(F32).reshape(1, -1)


def kernel(x, p, ev_w_in, ev_b_fgate, ev_dw_kernel, ev_dw_bias, ev_cnorm_g, ev_cnorm_b, ev_w_out,
           od_w_in, od_b_igate, od_b_fgate, od_w_out, ln_mix_g, ln_mix_b, w_up, w_down,
           ln_ffn_g, ln_ffn_b, w_ple, w_ple_gate):
    B, S, D = x.shape
    M = B * S
    x_f = x.reshape(M, D)
    p2 = p.reshape(DEPTH, M, D_PLE)
    w_up_b, w_down_b = w_up.astype(BF16), w_down.astype(BF16)
    w_ple_b, w_ple_gate_b = w_ple.astype(BF16), w_ple_gate.astype(BF16)

    def ffn_ple(layer, x_bf, x_res, with_bf16):
        y = _ffn(x_bf, w_up_b, w_down_b, layer)
        return _ln_ple(x_res, y, p2, _row(ln_ffn_g[layer]), _row(ln_ffn_b[layer]),
                       w_ple_gate_b, w_ple_b, layer, with_bf16)

    qkv_w = 3 * ATT_WIDTH
    w_main, w_gate_t = _split_w_in(ev_w_in[0], qkv_w, HEADS)
    z, g_t = _in_proj(x_f, w_main, w_gate_t, _col_scale(w_main.shape[1], 0, ATT_WIDTH, QK_SCALE * LOG2E))
    z3 = z.reshape(B, S, -1)
    f_bias = jnp.broadcast_to(ev_b_fgate[0].astype(F32)[:, None, None], (HEADS, 1, LANES))
    attn = _fox_attention(z3, g_t.reshape(GATE_PAD, B, S // LANES, LANES), f_bias)
    taps = jnp.broadcast_to(ev_dw_kernel[0].astype(F32)[:, None, :], (CONV_WIDTH, SUBLANES, CONV_CH))
    conv = _conformer_conv(z3, taps, jnp.broadcast_to(_row(ev_dw_bias[0]), (SUBLANES, CONV_CH)),
                           _row(ev_cnorm_g[0]), _row(ev_cnorm_b[0]))
    x_f, x_bf = _out_proj(attn.reshape(M, -1), 0, conv.reshape(M, -1), 0, ev_w_out[0].astype(BF16),
                          x_f, _row(ln_mix_g[0]), _row(ln_mix_b[0]))
    x_f, x_bf = ffn_ple(0, x_bf, x_f, True)

    qkv_w = 2 * HEADS * QK_DIM + D_MODEL
    w_main, w_gate_t = _split_w_in(od_w_in[0], qkv_w, 2 * HEADS)
    z, g_t = _in_proj(x_bf, w_main, w_gate_t, _col_scale(w_main.shape[1], ATT_WIDTH, 2 * ATT_WIDTH, QK_SCALE))
    z3 = z.reshape(B, S, -1)
    gate_rows = g_t[:2 * HEADS].reshape(2 * HEADS, B, S // ML_CHUNK, ML_CHUNK).transpose(1, 2, 0, 3)
    i_bias = jnp.broadcast_to(od_b_igate[0].astype(F32)[:, None], (HEADS, ML_CHUNK))
    f_bias = jnp.broadcast_to(od_b_fgate[0].astype(F32)[:, None], (HEADS, ML_CHUNK))
    hg = _mlstm(z3, gate_rows, i_bias, f_bias).reshape(M, D)
    x_f, x_bf = _out_proj(hg, 0, hg, 1, od_w_out[0].astype(BF16), x_f, _row(ln_mix_g[1]), _row(ln_mix_b[1]))
    (x_f,) = ffn_ple(1, x_bf, x_f, False)
    return x_f.reshape(B, S, D)
```

```python
import functools

import jax
import jax.numpy as jnp
from jax import lax
from jax.experimental import pallas as pl
from jax.experimental.pallas import tpu as pltpu

F32 = jnp.float32
BF16 = jnp.bfloat16

D_MODEL = 2048
DEPTH = 2
D_PLE = 256
D_FF = 4 * D_MODEL
LN_EPS = 1e-5
HEADS = 8
QK_DIM = 128
ATT_WIDTH = HEADS * QK_DIM
CONV_CH = D_MODEL - ATT_WIDTH
CONV_WIDTH = 31
ML_V_DIM = D_MODEL // HEADS
ALPHA = (2 * DEPTH) ** 0.25
QK_SCALE = QK_DIM ** -0.5
LOG2E = 1.4426950408889634

LANES = 128
SUBLANES = 8
GATE_PAD = LANES
CONV_HALO = 32
VMEM_LIMIT = 58 * 1024 * 1024

PROJ_TM, PROJ_TN = 2048, 1024
ATT_TQ = 4096
ATT_QW = 256
ATT_TK = 512
CONV_TS = 512
CONV_ROWS = 32
OUT_TM = 512
FFN_TM, FFN_TF, FFN_ROWS = 1024, 1024, 512
PLE_TM = 512
OUT_SPLIT = 4
PLE_SPLIT = 2
ML_CHUNK = 256
ML_TS = 1024
ML_UNROLL_CHUNKS = False
SPLIT_ROWS = 256
ML_HEAD_GROUP = 8


def _params(*sem):
    return pltpu.CompilerParams(dimension_semantics=sem, vmem_limit_bytes=VMEM_LIMIT)


def _layer_norm(t, g, b):
    mu = jnp.mean(t, axis=-1, keepdims=True)
    tc = t - mu
    var = jnp.mean(tc * tc, axis=-1, keepdims=True)
    return tc * lax.rsqrt(var + LN_EPS) * g + b


def _log_sigmoid(x):
    return -(jnp.maximum(-x, 0.0) + jnp.log1p(jnp.exp(-jnp.abs(x))))


def _sigmoid(x):
    return 1.0 / (1.0 + jnp.exp(-x))


def _split3(x):
    hi = x.astype(BF16)
    r1 = x - hi.astype(F32)
    mid = r1.astype(BF16)
    lo = (r1 - mid.astype(F32)).astype(BF16)
    return hi, mid, lo


def _dot(a, b):
    return jnp.dot(a, b, preferred_element_type=F32)


def _dot_f32_lhs(x, ones_rhs):
    hi, mid, lo = _split3(x)
    return _dot(hi, ones_rhs) + _dot(mid, ones_rhs) + _dot(lo, ones_rhs)


def _dot_f32_rhs(ones_lhs, x):
    hi, mid, lo = _split3(x)
    return _dot(ones_lhs, hi) + _dot(ones_lhs, mid) + _dot(ones_lhs, lo)


def _iota2(shape, axis):
    return lax.broadcasted_iota(jnp.int32, shape, axis)


def _proj_kernel(x_ref, w_ref, wgt_ref, sc_ref, z_ref, gt_ref, *maybe_xb_ref):
    first = pl.program_id(1) == 0
    if maybe_xb_ref:
        (xb_ref,) = maybe_xb_ref

        @pl.when(first)
        def _():
            xb_ref[...] = x_ref[...].astype(BF16)
    else:
        xb_ref = x_ref
    z_ref[...] = (_dot(xb_ref[...], w_ref[...]) * sc_ref[...]).astype(z_ref.dtype)

    @pl.when(first)
    def _():
        gt_ref[...] = lax.dot_general(wgt_ref[...], xb_ref[...], (((1,), (1,)), ((), ())),
                                      preferred_element_type=F32)


def _in_proj(x, w, wg_t, col_scale):
    m, k = x.shape
    n = w.shape[1]
    tn = PROJ_TN
    tm = PROJ_TM if x.dtype == BF16 else PROJ_TM // 2
    scratch = [] if x.dtype == BF16 else [pltpu.VMEM((tm, k), BF16)]
    return pl.pallas_call(
        _proj_kernel,
        grid=(m // tm, n // tn),
        in_specs=[
            pl.BlockSpec((tm, k), lambda i, j: (i, 0)),
            pl.BlockSpec((k, tn), lambda i, j: (0, j)),
            pl.BlockSpec((GATE_PAD, k), lambda i, j: (0, 0)),
            pl.BlockSpec((1, tn), lambda i, j: (0, j)),
        ],
        out_specs=[
            pl.BlockSpec((tm, tn), lambda i, j: (i, j)),
            pl.BlockSpec((GATE_PAD, tm), lambda i, j: (0, i)),
        ],
        out_shape=[
            jax.ShapeDtypeStruct((m, n), BF16),
            jax.ShapeDtypeStruct((GATE_PAD, m), F32),
        ],
        scratch_shapes=scratch,
        compiler_params=_params("parallel", "arbitrary"),
        name="in_proj",
    )(x, w, wg_t, col_scale)


def _attn_kernel(q_ref, k_ref, v_ref, f_ref, fb_ref, o_ref, kaug_ref, vt_ref, s0_ref, s1_ref, *state, tq):
    s_refs = (s0_ref, s1_ref)
    i = pl.program_id(2)
    qw, tk = ATT_QW, ATT_TK
    nsub = tq // qw
    blocks_per_tile = tk // qw
    tiles_per_q = tq // tk
    assert tiles_per_q % 2 == 0
    m_refs, l_refs, acc_refs = state[:nsub], state[nsub:2 * nsub], state[2 * nsub:]
    n_grp = f_ref.shape[2]

    def prepare_keys_values():
        f_rows = jnp.concatenate(
            [f_ref[0, 0] + fb_ref[0], jnp.zeros((LANES - n_grp, LANES), F32)], axis=0)
        ft = f_rows.T[:, 0:n_grp]
        logf = _log_sigmoid(ft)
        lower = (_iota2((LANES, LANES), 1) <= _iota2((LANES, LANES), 0)).astype(BF16)
        within = _dot_f32_rhs(lower, logf)
        tot = jnp.broadcast_to(within[LANES - 1:LANES, :], (2 * SUBLANES, n_grp))
        before = (_iota2((n_grp, n_grp), 0) < _iota2((n_grp, n_grp), 1)).astype(BF16)
        cum = (within + _dot_f32_lhs(tot, before)[0:1, :]) * LOG2E
        lane = _iota2((LANES, LANES), 1)
        for r in range(n_grp):
            col = jnp.broadcast_to(cum[:, r:r + 1], (LANES, LANES))
            hi = col.astype(BF16).astype(F32)
            rest = col - hi
            mid = rest.astype(BF16).astype(F32)
            lo = rest - mid
            terms = jnp.where(lane == 0, hi, jnp.where(lane == 1, mid, jnp.where(lane == 2, lo, 0.0)))
            kaug_ref[r * LANES:(r + 1) * LANES, QK_DIM:] = terms.astype(BF16)
        kaug_ref[:, :QK_DIM] = k_ref[0]
        for blk in range(vt_ref.shape[0]):
            vt_ref[blk] = v_ref[0, blk * qw:(blk + 1) * qw, :].T

    pl.when(i == 0)(prepare_keys_values)

    q_aug =jnp.where(_iota2((qw, LANES), 1) < 3, -1.0, 0.0).astype(BF16)
    q_t = [jnp.concatenate([q_ref[0, u * qw:(u + 1) * qw, :], q_aug], axis=1).T for u in range(nsub)]
    for u in range(nsub):
        m_refs[u][...] = jnp.full((1, qw), -jnp.inf, F32)
        l_refs[u][...] = jnp.zeros((1, qw), F32)
        acc_refs[u][...] = jnp.zeros((QK_DIM, qw), F32)

    def scores(t, slot, subs):
        kk = kaug_ref[pl.ds(pl.multiple_of(t * tk, tk), tk), :]
        for u in subs:
            s_refs[slot][u] = _dot(kk, q_t[u])

    def update(t, slot, u, mask):
        s = s_refs[slot][u]
        if mask is not None:
            s = jnp.where(mask, s, -jnp.inf)
        m_old = m_refs[u][...]
        m_new = jnp.maximum(m_old, jnp.max(s, axis=0, keepdims=True))
        alpha = jnp.exp2(m_old - m_new)
        p = jnp.exp2(s - m_new)
        l_refs[u][...] = alpha * l_refs[u][...] + jnp.sum(p, axis=0, keepdims=True)
        pb = p.astype(BF16)
        pv = _dot(vt_ref[t * blocks_per_tile], pb[0:qw, :])
        for n in range(1, blocks_per_tile):
            pv = pv + _dot(vt_ref[t * blocks_per_tile + n], pb[n * qw:(n + 1) * qw, :])
        acc_refs[u][...] = alpha * acc_refs[u][...] + pv
        m_refs[u][...] = m_new

    n_main = i * tiles_per_q
    scores(0, 0, range(nsub))

    def body(jj, carry):
        for half in range(2):
            t = 2 * jj + half
            scores(t + 1, 1 - half, range(nsub))
            for u in range(nsub):
                update(t, half, u, None)
        return carry

    lax.fori_loop(0, n_main // 2, body, 0)
    for d in range(tiles_per_q):
        if d + 1 < tiles_per_q:
            scores(n_main + d + 1, (d + 1) % 2, [u for u in range(nsub) if (u + 1) * qw > (d + 1) * tk])
        for u in range(nsub):
            if (u + 1) * qw <= d * tk:
                continue
            mask = None
            if u * qw < (d + 1) * tk - 1:
                mask = d * tk + _iota2((tk, qw), 0) <= u * qw + _iota2((tk, qw), 1)
            update(n_main + d, d % 2, u, mask)
    for u in range(nsub):
        out = (acc_refs[u][...] / l_refs[u][...]).T
        o_ref[0, u * qw:(u + 1) * qw, :] = out.astype(o_ref.dtype)


def _fox_attention(z3, f_rows, f_bias):
    b, s, _ = z3.shape
    tq = ATT_TQ
    nsub = tq // ATT_QW
    return pl.pallas_call(
        functools.partial(_attn_kernel, tq=tq),
        grid=(b, HEADS, s // tq),
        in_specs=[
            pl.BlockSpec((1, tq, QK_DIM), lambda bi, h, i: (bi, i, h)),
            pl.BlockSpec((1, s, QK_DIM), lambda bi, h, i: (bi, 0, HEADS + h)),
            pl.BlockSpec((1, s, QK_DIM), lambda bi, h, i: (bi, 0, 2 * HEADS + h)),
            pl.BlockSpec((1, 1, s // LANES, LANES), lambda bi, h, i: (h, bi, 0, 0)),
            pl.BlockSpec((1, 1, LANES), lambda bi, h, i: (h, 0, 0)),
        ],
        out_specs=pl.BlockSpec((1, tq, QK_DIM), lambda bi, h, i: (bi, i, h)),
        out_shape=jax.ShapeDtypeStruct((b, s, ATT_WIDTH), BF16),
        scratch_shapes=(
            [pltpu.VMEM((s, 2 * QK_DIM), BF16),
             pltpu.VMEM((s // ATT_QW, QK_DIM, ATT_QW), BF16),
             pltpu.VMEM((nsub, ATT_TK, ATT_QW), F32),
             pltpu.VMEM((nsub, ATT_TK, ATT_QW), F32)]
            + [pltpu.VMEM((1, ATT_QW), F32)] * (2 * nsub)
            + [pltpu.VMEM((QK_DIM, ATT_QW), F32)] * nsub),
        compiler_params=_params("parallel", "parallel", "arbitrary"),
        name="fox_attention",
    )(z3, z3, z3, f_rows, f_bias)


def _conv_kernel(a_ref, g_ref, kw_ref, kb_ref, ng_ref, nb_ref, o_ref, ybuf, conv_ref, *, ts):
    si = pl.program_id(1)

    @pl.when(si == 0)
    def _():
        for r in range(SUBLANES):
            ybuf[r, 0:CONV_HALO - r, :] = jnp.zeros((CONV_HALO - r, CONV_CH), F32)

    @pl.when(si > 0)
    def _():
        for r in range(SUBLANES):
            ybuf[r, 0:CONV_HALO - r, :] = ybuf[r, ts:ts + CONV_HALO - r, :]

    y = a_ref[0].astype(F32) * _sigmoid(g_ref[0].astype(F32))
    for r in range(SUBLANES):
        ybuf[r, CONV_HALO - r:CONV_HALO - r + ts, :] = y

    first = CONV_HALO - (CONV_WIDTH - 1)
    rows = CONV_ROWS

    def block(rb, carry):
        r0 = pl.multiple_of(rb * rows, rows)
        acc = [kb_ref[...]] * (rows // SUBLANES)
        for w in range(CONV_WIDTH):
            q8, r = divmod(first + w, SUBLANES)
            tap = kw_ref[w]
            for sb in range(rows // SUBLANES):
                acc[sb] = acc[sb] + ybuf[r, pl.ds(r0 + (q8 + sb) * SUBLANES, SUBLANES), :] * tap
        conv_ref[pl.ds(r0, rows), :] = jnp.concatenate(acc, axis=0)
        return carry

    lax.fori_loop(0, ts // rows, block, 0)
    t = _layer_norm(conv_ref[...], ng_ref[...], nb_ref[...])
    o_ref[0] = (t * _sigmoid(t)).astype(o_ref.dtype)


def _conformer_conv(z3, kw, kb, ng, nb):
    b, s, _ = z3.shape
    ts = CONV_TS
    a_blk = 3 * ATT_WIDTH // CONV_CH
    vec = pl.BlockSpec((1, CONV_CH), lambda bi, si: (0, 0))
    return pl.pallas_call(
        functools.partial(_conv_kernel, ts=ts),
        grid=(b, s // ts),
        in_specs=[
            pl.BlockSpec((1, ts, CONV_CH), lambda bi, si: (bi, si, a_blk)),
            pl.BlockSpec((1, ts, CONV_CH), lambda bi, si: (bi, si, a_blk + 1)),
            pl.BlockSpec((CONV_WIDTH, SUBLANES, CONV_CH), lambda bi, si: (0, 0, 0)),
            pl.BlockSpec((SUBLANES, CONV_CH), lambda bi, si: (0, 0)),
            vec, vec,
        ],
        out_specs=pl.BlockSpec((1, ts, CONV_CH), lambda bi, si: (bi, si, 0)),
        out_shape=jax.ShapeDtypeStruct((b, s, CONV_CH), BF16),
        scratch_shapes=[pltpu.VMEM((SUBLANES, ts + CONV_HALO, CONV_CH), F32),
                        pltpu.VMEM((ts, CONV_CH), F32)],
        compiler_params=_params("parallel", "arbitrary"),
        name="conformer_conv",
    )(z3, z3, kw, kb, ng, nb)


def _out_kernel(a1_ref, a2_ref, w1_ref, w2_ref, xr_ref, g_ref, b_ref, of_ref, ob_ref):
    tm = of_ref.shape[0]
    for c in range(OUT_SPLIT):
        rows = slice(c * tm // OUT_SPLIT, (c + 1) * tm // OUT_SPLIT)
        y = _dot(a1_ref[rows, :], w1_ref[...]) + _dot(a2_ref[rows, :], w2_ref[...])
        x = _layer_norm(ALPHA * xr_ref[rows, :] + y, g_ref[...], b_ref[...])
        of_ref[rows, :] = x
        ob_ref[rows, :] = x.astype(BF16)


def _out_proj(a1, a1_blk, a2, a2_blk, w, x_res, g, b):
    m, d = x_res.shape
    half = d // 2
    tm = OUT_TM
    vec = pl.BlockSpec((1, d), lambda i: (0, 0))
    return pl.pallas_call(
        _out_kernel,
        grid=(m // tm,),
        in_specs=[
            pl.BlockSpec((tm, half), lambda i: (i, a1_blk)),
            pl.BlockSpec((tm, half), lambda i: (i, a2_blk)),
            pl.BlockSpec((half, d), lambda i: (0, 0)),
            pl.BlockSpec((half, d), lambda i: (1, 0)),
            pl.BlockSpec((tm, d), lambda i: (i, 0)),
            vec, vec,
        ],
        out_specs=[pl.BlockSpec((tm, d), lambda i: (i, 0)), pl.BlockSpec((tm, d), lambda i: (i, 0))],
        out_shape=[jax.ShapeDtypeStruct((m, d), F32), jax.ShapeDtypeStruct((m, d), BF16)],
        compiler_params=_params("parallel"),
        name="out_proj_ln",
    )(a1, a2, w, w, x_res, g, b)


def _ffn_kernel(x_ref, wu_ref, wd_ref, o_ref):
    @pl.when(pl.program_id(1) == 0)
    def _():
        o_ref[...] = jnp.zeros(o_ref.shape, F32)

    for c in range(o_ref.shape[0] // FFN_ROWS):
        rows = slice(c * FFN_ROWS, (c + 1) * FFN_ROWS)
        h = jnp.maximum(_dot(x_ref[rows, :], wu_ref[...]), 0.0)
        o_ref[rows, :] += _dot((h * h).astype(BF16), wd_ref[...])


def _ffn(x_bf, w_up, w_down, layer):
    m, d = x_bf.shape
    dff = w_up.shape[2]
    tm, tf = FFN_TM, FFN_TF
    return pl.pallas_call(
        _ffn_kernel,
        grid=(m // tm, dff // tf),
        in_specs=[
            pl.BlockSpec((tm, d), lambda i, f: (i, 0)),
            pl.BlockSpec((None, d, tf), lambda i, f: (layer, 0, f)),
            pl.BlockSpec((None, tf, d), lambda i, f: (layer, f, 0)),
        ],
        out_specs=pl.BlockSpec((tm, d), lambda i, f: (i, 0)),
        out_shape=jax.ShapeDtypeStruct((m, d), F32),
        compiler_params=_params("parallel", "arbitrary"),
        name="ffn",
    )(x_bf, w_up, w_down)


def _ple_kernel(xr_ref, y_ref, p_ref, g_ref, b_ref, wg_ref, wp_ref, of_ref, *maybe_ob_ref):
    tm = of_ref.shape[0]
    for c in range(PLE_SPLIT):
        rows = slice(c * tm // PLE_SPLIT, (c + 1) * tm // PLE_SPLIT)
        x = _layer_norm(ALPHA * xr_ref[rows, :] + y_ref[rows, :], g_ref[...], b_ref[...])
        gate = _sigmoid(_dot(x.astype(BF16), wg_ref[...]))
        out = x + gate * _dot(p_ref[rows, :].astype(BF16), wp_ref[...])
        of_ref[rows, :] = out
        for ob_ref in maybe_ob_ref:
            ob_ref[rows, :] = out.astype(BF16)


def _ln_ple(x_res, y, p, g, b, wg, wp, layer, with_bf16):
    m, d = x_res.shape
    tm = PLE_TM
    row = pl.BlockSpec((tm, d), lambda i: (i, 0))
    vec = pl.BlockSpec((1, d), lambda i: (0, 0))
    once = pl.Buffered(1)
    out_specs = [row]
    out_shape = [jax.ShapeDtypeStruct((m, d), F32)]
    if with_bf16:
        out_specs.append(row)
        out_shape.append(jax.ShapeDtypeStruct((m, d), BF16))
    return pl.pallas_call(
        _ple_kernel,
        grid=(m // tm,),
        in_specs=[
            row, row,
            pl.BlockSpec((None, tm, D_PLE), lambda i: (layer, i, 0)),
            vec, vec,
            pl.BlockSpec((None, d, d), lambda i: (layer, 0, 0), pipeline_mode=once),
            pl.BlockSpec((None, D_PLE, d), lambda i: (layer, 0, 0), pipeline_mode=once),
        ],
        out_specs=out_specs,
        out_shape=out_shape,
        compiler_params=_params("parallel"),
        name="ln_ple_gate",
    )(x_res, y, p, g, b, wg, wp)


def _mlstm_kernel(q_ref, k_ref, v_ref, o_ref, g_ref, ib_ref, fb_ref, out_ref,
                  b_ref, r_ref, *state, chunk):
    L = chunk
    nck = q_ref.shape[1] // L
    c_refs, n_refs, m_refs = state[:HEADS], state[HEADS:2 * HEADS], state[2 * HEADS:]
    tn = (((0,), (0,)), ((), ()))
    nt = (((1,), (1,)), ((), ()))

    @pl.when(pl.program_id(1) == 0)
    def _():
        for h in range(HEADS):
            c_refs[h][...] = jnp.zeros((QK_DIM, ML_V_DIM), F32)
            n_refs[h][...] = jnp.zeros((1, QK_DIM), F32)
            m_refs[h][...] = jnp.zeros((1, 1), F32)

    upper = (_iota2((L, L), 0) <= _iota2((L, L), 1)).astype(BF16)
    for c in range(nck):
        gates = g_ref[0, c]
        logf = _log_sigmoid(gates[HEADS:] + fb_ref[...])
        b_c = _dot_f32_lhs(jnp.concatenate([logf, logf], axis=0), upper)[0:HEADS]
        b_ref[c] = b_c
        r_ref[c] = gates[:HEADS] + ib_ref[...] - b_c

    sub16 = _iota2((2 * SUBLANES, L), 0)
    sub_c = _iota2((2 * SUBLANES, 2 * LANES), 0)
    lane_c = _iota2((2 * SUBLANES, 2 * LANES), 1)
    rhs_const = jnp.where(((sub_c < 3) & (lane_c < LANES)) | ((sub_c >= 3) & (sub_c < 6) & (lane_c >= LANES)),
                          1.0, 0.0).astype(BF16)
    causal = _iota2((L, L), 1) <= _iota2((L, L), 0)

    def split_rows(x):
        hi = x.astype(BF16).astype(F32)
        rest = x - hi
        mid = rest.astype(BF16).astype(F32)
        return hi, mid, rest - mid

    def step(c, carry, first_head):
        r0 = c * L if isinstance(c, int) else pl.multiple_of(c * L, L)
        b_all = b_ref[c]
        r_all = r_ref[c]
        for h in range(first_head, first_head + ML_HEAD_GROUP):
            q = q_ref[0, pl.ds(r0, L), h * QK_DIM:(h + 1) * QK_DIM]
            k = k_ref[0, pl.ds(r0, L), h * QK_DIM:(h + 1) * QK_DIM]
            v = v_ref[0, pl.ds(r0, L), h * ML_V_DIM:(h + 1) * ML_V_DIM]
            b_row = b_all[h:h + 1, :]
            r_row = r_all[h:h + 1, :]
            bh, bm, bl = split_rows(b_row)
            rh, rm, rl = split_rows(r_row)
            lhs = jnp.where(sub16 == 0, bh, jnp.where(sub16 == 1, bm, jnp.where(sub16 == 2, bl,
                  jnp.where(sub16 == 3, rh, jnp.where(sub16 == 4, rm, jnp.where(sub16 == 5, rl,
                  jnp.where(sub16 < 9, 1.0, 0.0))))))).astype(BF16)
            rhs_d = jnp.where(sub16 < 3, 1.0, jnp.where(sub16 == 6, rh, jnp.where(sub16 == 7, rm,
                    jnp.where(sub16 == 8, rl, 0.0)))).astype(BF16)
            rhs = jnp.concatenate([rhs_d, rhs_const], axis=1)
            cols = lax.dot_general(lhs, rhs, tn, preferred_element_type=F32)
            b_col = cols[:, L:L + LANES]
            r_col = cols[:, L + LANES:]
            m_prev = m_refs[h][...]

            d_intra = jnp.where(causal, cols[:, 0:L], -jnp.inf)
            d_inter = b_col + m_prev
            m_t = jnp.maximum(d_inter, jnp.max(d_intra, axis=1, keepdims=True))
            w_intra = jnp.exp(d_intra - jnp.concatenate([m_t] * (L // LANES), axis=1))
            w_inter = jnp.exp(d_inter - m_t)
            s = lax.dot_general(q, k, nt, preferred_element_type=F32) * w_intra
            c_old = c_refs[h][...]
            n_old = n_refs[h][...]
            num = (jnp.concatenate([w_inter] * (ML_V_DIM // LANES), axis=1) * _dot(q, c_old.astype(BF16))
                   + _dot(s.astype(BF16), v))
            den = (w_inter[:, 0:1] * jnp.sum(q.astype(F32) * n_old, axis=1, keepdims=True)
                   + jnp.sum(s, axis=1, keepdims=True))
            hval = num / jnp.maximum(jnp.abs(den), jnp.exp(-m_t[:, 0:1]))
            gate = _sigmoid(o_ref[0, pl.ds(r0, L), h * ML_V_DIM:(h + 1) * ML_V_DIM].astype(F32))
            out_ref[0, pl.ds(r0, L), h * ML_V_DIM:(h + 1) * ML_V_DIM] = (gate * hval).astype(out_ref.dtype)

            b_last = b_row[:, L - 1:L]
            m_new = jnp.maximum(b_last + m_prev, jnp.max(b_last + r_row, axis=1, keepdims=True))
            decay = jnp.exp(b_last + m_prev - m_new)
            wk = jnp.exp(b_last + r_col - m_new) * k.astype(F32)
            c_refs[h][...] = decay * c_old + lax.dot_general(wk.astype(BF16), v, tn, preferred_element_type=F32)
            n_refs[h][...] = decay * n_old + jnp.sum(wk, axis=0, keepdims=True)
            m_refs[h][...] = m_new
        return carry

    for first_head in range(0, HEADS, ML_HEAD_GROUP):
        if ML_UNROLL_CHUNKS:
            for c in range(nck):
                step(c, 0, first_head)
        else:
            lax.fori_loop(0, nck, functools.partial(step, first_head=first_head), 0)


def _mlstm(z3, gate_rows, i_bias, f_bias):
    b, s, _ = z3.shape
    L = ML_CHUNK
    ts = ML_TS
    nck = ts // L
    qk_w = HEADS * QK_DIM
    gate_spec = pl.BlockSpec((1, nck, 2 * HEADS, L), lambda bi, si: (bi, si, 0, 0))
    bias_spec = pl.BlockSpec((HEADS, L), lambda bi, si: (0, 0))
    return pl.pallas_call(
        functools.partial(_mlstm_kernel, chunk=L),
        grid=(b, s // ts),
        in_specs=[
            pl.BlockSpec((1, ts, qk_w), lambda bi, si: (bi, si, 0)),
            pl.BlockSpec((1, ts, qk_w), lambda bi, si: (bi, si, 1)),
            pl.BlockSpec((1, ts, D_MODEL), lambda bi, si: (bi, si, 1)),
            pl.BlockSpec((1, ts, D_MODEL), lambda bi, si: (bi, si, 2)),
            gate_spec, bias_spec, bias_spec,
        ],
        out_specs=pl.BlockSpec((1, ts, D_MODEL), lambda bi, si: (bi, si, 0)),
        out_shape=jax.ShapeDtypeStruct((b, s, D_MODEL), BF16),
        scratch_shapes=(
            [pltpu.VMEM((nck, HEADS, L), F32)] * 2
            + [pltpu.VMEM((QK_DIM, ML_V_DIM), F32)] * HEADS
            + [pltpu.VMEM((1, QK_DIM), F32)] * HEADS
            + [pltpu.VMEM((1, 1), F32)] * HEADS),
        compiler_params=_params("parallel", "arbitrary"),
        name="mlstm",
    )(z3, z3, z3, z3, gate_rows, i_bias, f_bias)


def _split_kernel(w_ref, main_ref, gate_ref, *, gate_lo, n_gate):
    main_ref[:, :gate_lo] = w_ref[:, :gate_lo].astype(BF16)
    main_ref[:, gate_lo:] = w_ref[:, gate_lo + n_gate:].astype(BF16)
    gate_cols = w_ref[:, gate_lo:gate_lo + GATE_PAD].T
    keep = _iota2(gate_cols.shape, 0) < n_gate
    gate_ref[...] = jnp.where(keep, gate_cols, 0.0).astype(BF16)


def _split_w_in(w, gate_lo, n_gate):
    k, n = w.shape
    tr = SPLIT_ROWS
    return pl.pallas_call(
        functools.partial(_split_kernel, gate_lo=gate_lo, n_gate=n_gate),
        grid=(k // tr,),
        in_specs=[pl.BlockSpec((tr, n), lambda i: (i, 0))],
        out_specs=[pl.BlockSpec((tr, n - n_gate), lambda i: (i, 0)),
                   pl.BlockSpec((GATE_PAD, tr), lambda i: (0, i))],
        out_shape=[jax.ShapeDtypeStruct((k, n - n_gate), BF16),
                   jax.ShapeDtypeStruct((GATE_PAD, k), BF16)],
        compiler_params=_params("parallel"),
        name="split_w_in",
    )(w)


def _col_scale(n, *ranges):
    col = jnp.arange(n)[None, :]
    scale = jnp.ones((1, n), F32)
    for lo, hi, value in ranges:
        scale = jnp.where((col >= lo) & (col < hi), value, scale)
    return scale


def _row(v):
    return v.astype(F32).reshape(1, -1)


def kernel(x, p, ev_w_in, ev_b_fgate, ev_dw_kernel, ev_dw_bias, ev_cnorm_g, ev_cnorm_b, ev_w_out,
           od_w_in, od_b_igate, od_b_fgate, od_w_out, ln_mix_g, ln_mix_b, w_up, w_down,
           ln_ffn_g, ln_ffn_b, w_ple, w_ple_gate):
    B, S, D = x.shape
    M = B * S
    x_f = x.reshape(M, D)
    p2 = p.reshape(DEPTH, M, D_PLE)
    w_up_b, w_down_b = w_up.astype(BF16), w_down.astype(BF16)
    w_ple_b, w_ple_gate_b = w_ple.astype(BF16), w_ple_gate.astype(BF16)

    def ffn_ple(layer, x_bf, x_res, with_bf16):
        y = _ffn(x_bf, w_up_b, w_down_b, layer)
        return _ln_ple(x_res, y, p2, _row(ln_ffn_g[layer]), _row(ln_ffn_b[layer]),
                       w_ple_gate_b, w_ple_b, layer, with_bf16)

    qkv_w = 3 * ATT_WIDTH
    w_main, w_gate_t = _split_w_in(ev_w_in[0], qkv_w, HEADS)
    n_main = w_main.shape[1]
    z, g_t = _in_proj(x_f, w_main, w_gate_t, _col_scale(
        n_main, (0, ATT_WIDTH, QK_SCALE * LOG2E)))
    z3 = z.reshape(B, S, -1)
    f_bias = jnp.broadcast_to(ev_b_fgate[0].astype(F32)[:, None, None], (HEADS, 1, LANES))
    attn = _fox_attention(z3, g_t.reshape(GATE_PAD, B, S // LANES, LANES), f_bias)
    taps = jnp.broadcast_to(ev_dw_kernel[0].astype(F32)[:, None, :], (CONV_WIDTH, SUBLANES, CONV_CH))
    conv = _conformer_conv(z3, taps, jnp.broadcast_to(_row(ev_dw_bias[0]), (SUBLANES, CONV_CH)),
                           _row(ev_cnorm_g[0]), _row(ev_cnorm_b[0]))
    x_f, x_bf = _out_proj(attn.reshape(M, -1), 0, conv.reshape(M, -1), 0, ev_w_out[0].astype(BF16),
                          x_f, _row(ln_mix_g[0]), _row(ln_mix_b[0]))
    x_f, x_bf = ffn_ple(0, x_bf, x_f, True)

    qkv_w = 2 * HEADS * QK_DIM + D_MODEL
    w_main, w_gate_t = _split_w_in(od_w_in[0], qkv_w, 2 * HEADS)
    n_main = w_main.shape[1]
    z, g_t = _in_proj(x_bf, w_main, w_gate_t, _col_scale(
        n_main, (ATT_WIDTH, 2 * ATT_WIDTH, QK_SCALE)))
    z3 = z.reshape(B, S, -1)
    gate_rows = g_t[:2 * HEADS].reshape(2 * HEADS, B, S // ML_CHUNK, ML_CHUNK).transpose(1, 2, 0, 3)
    i_bias = jnp.broadcast_to(od_b_igate[0].astype(F32)[:, None], (HEADS, ML_CHUNK))
    f_bias = jnp.broadcast_to(od_b_fgate[0].astype(F32)[:, None], (HEADS, ML_CHUNK))
    hg = _mlstm(z3, gate_rows, i_bias, f_bias).reshape(M, D)
    x_f, x_bf = _out_proj(hg, 0, hg, 1, od_w_out[0].astype(BF16), x_f, _row(ln_mix_g[1]), _row(ln_mix_b[1]))
    (x_f,) = ffn_ple(1, x_bf, x_f, False)
    return x_f.reshape(B, S, D)
```

```python
import functools

import jax
import jax.numpy as jnp
from jax import lax
from jax.experimental import pallas as pl
from jax.experimental.pallas import tpu as pltpu

F32 = jnp.float32
BF16 = jnp.bfloat16

D_MODEL = 2048
DEPTH = 2
D_PLE = 256
D_FF = 4 * D_MODEL
LN_EPS = 1e-5
HEADS = 8
QK_DIM = 128
ATT_WIDTH = HEADS * QK_DIM
CONV_CH = D_MODEL - ATT_WIDTH
CONV_WIDTH = 31
ML_V_DIM = D_MODEL // HEADS
ALPHA = (2 * DEPTH) ** 0.25
QK_SCALE = QK_DIM ** -0.5
LOG2E = 1.4426950408889634

LANES = 128
SUBLANES = 8
GATE_PAD = LANES
CONV_HALO = 32
VMEM_LIMIT = 58 * 1024 * 1024

PROJ_TM, PROJ_TN = 2048, 1024
ATT_TQ = 4096
ATT_QW = 256
ATT_TK = 512
CONV_TS = 512
CONV_ROWS = 32
OUT_TM = 512
FFN_TM, FFN_TF, FFN_ROWS = 1024, 1024, 512
PLE_TM = 512
OUT_SPLIT = 4
PLE_SPLIT = 2
ML_CHUNK = 256
ML_TS = 1024
ML_UNROLL_CHUNKS = False
ML_HEAD_GROUP = 8


def _params(*sem):
    return pltpu.CompilerParams(dimension_semantics=sem, vmem_limit_bytes=VMEM_LIMIT)


def _layer_norm(t, g, b):
    mu = jnp.mean(t, axis=-1, keepdims=True)
    tc = t - mu
    var = jnp.mean(tc * tc, axis=-1, keepdims=True)
    return tc * lax.rsqrt(var + LN_EPS) * g + b


def _log_sigmoid(x):
    return -(jnp.maximum(-x, 0.0) + jnp.log1p(jnp.exp(-jnp.abs(x))))


def _sigmoid(x):
    return 1.0 / (1.0 + jnp.exp(-x))


def _split3(x):
    hi = x.astype(BF16)
    r1 = x - hi.astype(F32)
    mid = r1.astype(BF16)
    lo = (r1 - mid.astype(F32)).astype(BF16)
    return hi, mid, lo


def _dot(a, b):
    return jnp.dot(a, b, preferred_element_type=F32)


def _dot_f32_lhs(x, ones_rhs):
    hi, mid, lo = _split3(x)
    return _dot(hi, ones_rhs) + _dot(mid, ones_rhs) + _dot(lo, ones_rhs)


def _dot_f32_rhs(ones_lhs, x):
    hi, mid, lo = _split3(x)
    return _dot(ones_lhs, hi) + _dot(ones_lhs, mid) + _dot(ones_lhs, lo)


def _iota2(shape, axis):
    return lax.broadcasted_iota(jnp.int32, shape, axis)


def _proj_kernel(x_ref, wt_ref, wgt_ref, sc_ref, z_ref, gt_ref, *maybe_xb_ref):
    first = pl.program_id(1) == 0
    if maybe_xb_ref:
        (xb_ref,) = maybe_xb_ref

        @pl.when(first)
        def _():
            xb_ref[...] = x_ref[...].astype(BF16)
    else:
        xb_ref = x_ref
    nt = (((1,), (1,)), ((), ()))
    z = lax.dot_general(xb_ref[...], wt_ref[...].astype(BF16), nt, preferred_element_type=F32)
    z_ref[...] = (z * sc_ref[...]).astype(z_ref.dtype)

    @pl.when(first)
    def _():
        gt_ref[...] = lax.dot_general(wgt_ref[...].astype(BF16), xb_ref[...], nt, preferred_element_type=F32)


def _in_proj(x, w_t, gate_lo, n_gate, col_scale):
    m, k = x.shape
    n = w_t.shape[0] - n_gate
    tn = PROJ_TN
    assert gate_lo % tn == 0 and gate_lo % GATE_PAD == 0 and n % tn == 0

    def w_row(j):
        groups = j * (tn // SUBLANES) + jnp.where(j * tn >= gate_lo, n_gate // SUBLANES, 0)
        return groups * SUBLANES

    tm = PROJ_TM if x.dtype == BF16 else PROJ_TM // 2
    scratch = [] if x.dtype == BF16 else [pltpu.VMEM((tm, k), BF16)]
    return pl.pallas_call(
        _proj_kernel,
        grid=(m // tm, n // tn),
        in_specs=[
            pl.BlockSpec((tm, k), lambda i, j: (i, 0)),
            pl.BlockSpec((pl.Element(tn), pl.Element(k)), lambda i, j: (w_row(j), 0)),
            pl.BlockSpec((GATE_PAD, k), lambda i, j: (gate_lo // GATE_PAD, 0)),
            pl.BlockSpec((1, tn), lambda i, j: (0, j)),
        ],
        out_specs=[
            pl.BlockSpec((tm, tn), lambda i, j: (i, j)),
            pl.BlockSpec((GATE_PAD, tm), lambda i, j: (0, i)),
        ],
        out_shape=[
            jax.ShapeDtypeStruct((m, n), BF16),
            jax.ShapeDtypeStruct((GATE_PAD, m), F32),
        ],
        scratch_shapes=scratch,
        compiler_params=_params("parallel", "arbitrary"),
        name="in_proj",
    )(x, w_t, w_t, col_scale)


def _attn_kernel(q_ref, k_ref, v_ref, f_ref, fb_ref, o_ref, kaug_ref, vt_ref, s0_ref, s1_ref, *state, tq):
    s_refs = (s0_ref, s1_ref)
    i = pl.program_id(2)
    qw, tk = ATT_QW, ATT_TK
    nsub = tq // qw
    blocks_per_tile = tk // qw
    tiles_per_q = tq // tk
    assert tiles_per_q % 2 == 0
    m_refs, l_refs, acc_refs = state[:nsub], state[nsub:2 * nsub], state[2 * nsub:]
    n_grp = f_ref.shape[2]

    def prepare_keys_values():
        f_rows = jnp.concatenate(
            [f_ref[0, 0] + fb_ref[0], jnp.zeros((LANES - n_grp, LANES), F32)], axis=0)
        ft = f_rows.T[:, 0:n_grp]
        logf = _log_sigmoid(ft)
        lower = (_iota2((LANES, LANES), 1) <= _iota2((LANES, LANES), 0)).astype(BF16)
        within = _dot_f32_rhs(lower, logf)
        tot = jnp.broadcast_to(within[LANES - 1:LANES, :], (2 * SUBLANES, n_grp))
        before = (_iota2((n_grp, n_grp), 0) < _iota2((n_grp, n_grp), 1)).astype(BF16)
        cum = (within + _dot_f32_lhs(tot, before)[0:1, :]) * LOG2E
        lane = _iota2((LANES, LANES), 1)
        for r in range(n_grp):
            col = jnp.broadcast_to(cum[:, r:r + 1], (LANES, LANES))
            hi = col.astype(BF16).astype(F32)
            rest = col - hi
            mid = rest.astype(BF16).astype(F32)
            lo = rest - mid
            terms = jnp.where(lane == 0, hi, jnp.where(lane == 1, mid, jnp.where(lane == 2, lo, 0.0)))
            kaug_ref[r * LANES:(r + 1) * LANES, QK_DIM:] = terms.astype(BF16)
        kaug_ref[:, :QK_DIM] = k_ref[0]
        for blk in range(vt_ref.shape[0]):
            vt_ref[blk] = v_ref[0, blk * qw:(blk + 1) * qw, :].T

    pl.when(i == 0)(prepare_keys_values)

    q_aug =jnp.where(_iota2((qw, LANES), 1) < 3, -1.0, 0.0).astype(BF16)
    q_t = [jnp.concatenate([q_ref[0, u * qw:(u + 1) * qw, :], q_aug], axis=1).T for u in range(nsub)]
    for u in range(nsub):
        m_refs[u][...] = jnp.full((1, qw), -jnp.inf, F32)
        l_refs[u][...] = jnp.zeros((1, qw), F32)
        acc_refs[u][...] = jnp.zeros((QK_DIM, qw), F32)

    def scores(t, slot, subs):
        kk = kaug_ref[pl.ds(pl.multiple_of(t * tk, tk), tk), :]
        for u in subs:
            s_refs[slot][u] = _dot(kk, q_t[u])

    def update(t, slot, u, mask):
        s = s_refs[slot][u]
        if mask is not None:
            s = jnp.where(mask, s, -jnp.inf)
        m_old = m_refs[u][...]
        m_new = jnp.maximum(m_old, jnp.max(s, axis=0, keepdims=True))
        alpha = jnp.exp2(m_old - m_new)
        p = jnp.exp2(s - m_new)
        l_refs[u][...] = alpha * l_refs[u][...] + jnp.sum(p, axis=0, keepdims=True)
        pb = p.astype(BF16)
        pv = _dot(vt_ref[t * blocks_per_tile], pb[0:qw, :])
        for n in range(1, blocks_per_tile):
            pv = pv + _dot(vt_ref[t * blocks_per_tile + n], pb[n * qw:(n + 1) * qw, :])
        acc_refs[u][...] = alpha * acc_refs[u][...] + pv
        m_refs[u][...] = m_new

    n_main = i * tiles_per_q
    scores(0, 0, range(nsub))

    def body(jj, carry):
        for half in range(2):
            t = 2 * jj + half
            scores(t + 1, 1 - half, range(nsub))
            for u in range(nsub):
                update(t, half, u, None)
        return carry

    lax.fori_loop(0, n_main // 2, body, 0)
    for d in range(tiles_per_q):
        if d + 1 < tiles_per_q:
            scores(n_main + d + 1, (d + 1) % 2, [u for u in range(nsub) if (u + 1) * qw > (d + 1) * tk])
        for u in range(nsub):
            if (u + 1) * qw <= d * tk:
                continue
            mask = None
            if u * qw < (d + 1) * tk - 1:
                mask = d * tk + _iota2((tk, qw), 0) <= u * qw + _iota2((tk, qw), 1)
            update(n_main + d, d % 2, u, mask)
    for u in range(nsub):
        out = (acc_refs[u][...] / l_refs[u][...]).T
        o_ref[0, u * qw:(u + 1) * qw, :] = out.astype(o_ref.dtype)


def _fox_attention(z3, f_rows, f_bias):
    b, s, _ = z3.shape
    tq = ATT_TQ
    nsub = tq // ATT_QW
    return pl.pallas_call(
        functools.partial(_attn_kernel, tq=tq),
        grid=(b, HEADS, s // tq),
        in_specs=[
            pl.BlockSpec((1, tq, QK_DIM), lambda bi, h, i: (bi, i, h)),
            pl.BlockSpec((1, s, QK_DIM), lambda bi, h, i: (bi, 0, HEADS + h)),
            pl.BlockSpec((1, s, QK_DIM), lambda bi, h, i: (bi, 0, 2 * HEADS + h)),
            pl.BlockSpec((1, 1, s // LANES, LANES), lambda bi, h, i: (h, bi, 0, 0)),
            pl.BlockSpec((1, 1, LANES), lambda bi, h, i: (h, 0, 0)),
        ],
        out_specs=pl.BlockSpec((1, tq, QK_DIM), lambda bi, h, i: (bi, i, h)),
        out_shape=jax.ShapeDtypeStruct((b, s, ATT_WIDTH), BF16),
        scratch_shapes=(
            [pltpu.VMEM((s, 2 * QK_DIM), BF16),
             pltpu.VMEM((s // ATT_QW, QK_DIM, ATT_QW), BF16),
             pltpu.VMEM((nsub, ATT_TK, ATT_QW), F32),
             pltpu.VMEM((nsub, ATT_TK, ATT_QW), F32)]
            + [pltpu.VMEM((1, ATT_QW), F32)] * (2 * nsub)
            + [pltpu.VMEM((QK_DIM, ATT_QW), F32)] * nsub),
        compiler_params=_params("parallel", "parallel", "arbitrary"),
        name="fox_attention",
    )(z3, z3, z3, f_rows, f_bias)


def _conv_kernel(a_ref, g_ref, kw_ref, kb_ref, ng_ref, nb_ref, o_ref, ybuf, conv_ref, *, ts):
    si = pl.program_id(1)

    @pl.when(si == 0)
    def _():
        for r in range(SUBLANES):
            ybuf[r, 0:CONV_HALO - r, :] = jnp.zeros((CONV_HALO - r, CONV_CH), F32)

    @pl.when(si > 0)
    def _():
        for r in range(SUBLANES):
            ybuf[r, 0:CONV_HALO - r, :] = ybuf[r, ts:ts + CONV_HALO - r, :]

    y = a_ref[0].astype(F32) * _sigmoid(g_ref[0].astype(F32))
    for r in range(SUBLANES):
        ybuf[r, CONV_HALO - r:CONV_HALO - r + ts, :] = y

    first = CONV_HALO - (CONV_WIDTH - 1)
    rows = CONV_ROWS

    def block(rb, carry):
        r0 = pl.multiple_of(rb * rows, rows)
        acc = [kb_ref[...]] * (rows // SUBLANES)
        for w in range(CONV_WIDTH):
            q8, r = divmod(first + w, SUBLANES)
            tap = kw_ref[w]
            for sb in range(rows // SUBLANES):
                acc[sb] = acc[sb] + ybuf[r, pl.ds(r0 + (q8 + sb) * SUBLANES, SUBLANES), :] * tap
        conv_ref[pl.ds(r0, rows), :] = jnp.concatenate(acc, axis=0)
        return carry

    lax.fori_loop(0, ts // rows, block, 0)
    t = _layer_norm(conv_ref[...], ng_ref[...], nb_ref[...])
    o_ref[0] = (t * _sigmoid(t)).astype(o_ref.dtype)


def _conformer_conv(z3, kw, kb, ng, nb):
    b, s, _ = z3.shape
    ts = CONV_TS
    a_blk = 3 * ATT_WIDTH // CONV_CH
    vec = pl.BlockSpec((1, CONV_CH), lambda bi, si: (0, 0))
    return pl.pallas_call(
        functools.partial(_conv_kernel, ts=ts),
        grid=(b, s // ts),
        in_specs=[
            pl.BlockSpec((1, ts, CONV_CH), lambda bi, si: (bi, si, a_blk)),
            pl.BlockSpec((1, ts, CONV_CH), lambda bi, si: (bi, si, a_blk + 1)),
            pl.BlockSpec((CONV_WIDTH, SUBLANES, CONV_CH), lambda bi, si: (0, 0, 0)),
            pl.BlockSpec((SUBLANES, CONV_CH), lambda bi, si: (0, 0)),
            vec, vec,
        ],
        out_specs=pl.BlockSpec((1, ts, CONV_CH), lambda bi, si: (bi, si, 0)),
        out_shape=jax.ShapeDtypeStruct((b, s, CONV_CH), BF16),
        scratch_shapes=[pltpu.VMEM((SUBLANES, ts + CONV_HALO, CONV_CH), F32),
                        pltpu.VMEM((ts, CONV_CH), F32)],
        compiler_params=_params("parallel", "arbitrary"),
        name="conformer_conv",
    )(z3, z3, kw, kb, ng, nb)


def _out_kernel(a1_ref, a2_ref, w1_ref, w2_ref, xr_ref, g_ref, b_ref, of_ref, ob_ref):
    tm = of_ref.shape[0]
    for c in range(OUT_SPLIT):
        rows = slice(c * tm // OUT_SPLIT, (c + 1) * tm // OUT_SPLIT)
        y = _dot(a1_ref[rows, :], w1_ref[...]) + _dot(a2_ref[rows, :], w2_ref[...])
        x = _layer_norm(ALPHA * xr_ref[rows, :] + y, g_ref[...], b_ref[...])
        of_ref[rows, :] = x
        ob_ref[rows, :] = x.astype(BF16)


def _out_proj(a1, a1_blk, a2, a2_blk, w, x_res, g, b):
    m, d = x_res.shape
    half = d // 2
    tm = OUT_TM
    vec = pl.BlockSpec((1, d), lambda i: (0, 0))
    return pl.pallas_call(
        _out_kernel,
        grid=(m // tm,),
        in_specs=[
            pl.BlockSpec((tm, half), lambda i: (i, a1_blk)),
            pl.BlockSpec((tm, half), lambda i: (i, a2_blk)),
            pl.BlockSpec((half, d), lambda i: (0, 0)),
            pl.BlockSpec((half, d), lambda i: (1, 0)),
            pl.BlockSpec((tm, d), lambda i: (i, 0)),
            vec, vec,
        ],
        out_specs=[pl.BlockSpec((tm, d), lambda i: (i, 0)), pl.BlockSpec((tm, d), lambda i: (i, 0))],
        out_shape=[jax.ShapeDtypeStruct((m, d), F32), jax.ShapeDtypeStruct((m, d), BF16)],
        compiler_params=_params("parallel"),
        name="out_proj_ln",
    )(a1, a2, w, w, x_res, g, b)


def _ffn_kernel(x_ref, wu_ref, wd_ref, o_ref):
    @pl.when(pl.program_id(1) == 0)
    def _():
        o_ref[...] = jnp.zeros(o_ref.shape, F32)

    for c in range(o_ref.shape[0] // FFN_ROWS):
        rows = slice(c * FFN_ROWS, (c + 1) * FFN_ROWS)
        h = jnp.maximum(_dot(x_ref[rows, :], wu_ref[...]), 0.0)
        o_ref[rows, :] += _dot((h * h).astype(BF16), wd_ref[...])


def _ffn(x_bf, w_up, w_down, layer):
    m, d = x_bf.shape
    dff = w_up.shape[2]
    tm, tf = FFN_TM, FFN_TF
    return pl.pallas_call(
        _ffn_kernel,
        grid=(m // tm, dff // tf),
        in_specs=[
            pl.BlockSpec((tm, d), lambda i, f: (i, 0)),
            pl.BlockSpec((None, d, tf), lambda i, f: (layer, 0, f)),
            pl.BlockSpec((None, tf, d), lambda i, f: (layer, f, 0)),
        ],
        out_specs=pl.BlockSpec((tm, d), lambda i, f: (i, 0)),
        out_shape=jax.ShapeDtypeStruct((m, d), F32),
        compiler_params=_params("parallel", "arbitrary"),
        name="ffn",
    )(x_bf, w_up, w_down)


def _ple_kernel(xr_ref, y_ref, p_ref, g_ref, b_ref, wg_ref, wp_ref, of_ref, *maybe_ob_ref):
    tm = of_ref.shape[0]
    for c in range(PLE_SPLIT):
        rows = slice(c * tm // PLE_SPLIT, (c + 1) * tm // PLE_SPLIT)
        x = _layer_norm(ALPHA * xr_ref[rows, :] + y_ref[rows, :], g_ref[...], b_ref[...])
        gate = _sigmoid(_dot(x.astype(BF16), wg_ref[...]))
        out = x + gate * _dot(p_ref[rows, :].astype(BF16), wp_ref[...])
        of_ref[rows, :] = out
        for ob_ref in maybe_ob_ref:
            ob_ref[rows, :] = out.astype(BF16)


def _ln_ple(x_res, y, p, g, b, wg, wp, layer, with_bf16):
    m, d = x_res.shape
    tm = PLE_TM
    row = pl.BlockSpec((tm, d), lambda i: (i, 0))
    vec = pl.BlockSpec((1, d), lambda i: (0, 0))
    once = pl.Buffered(1)
    out_specs = [row]
    out_shape = [jax.ShapeDtypeStruct((m, d), F32)]
    if with_bf16:
        out_specs.append(row)
        out_shape.append(jax.ShapeDtypeStruct((m, d), BF16))
    return pl.pallas_call(
        _ple_kernel,
        grid=(m // tm,),
        in_specs=[
            row, row,
            pl.BlockSpec((None, tm, D_PLE), lambda i: (layer, i, 0)),
            vec, vec,
            pl.BlockSpec((None, d, d), lambda i: (layer, 0, 0), pipeline_mode=once),
            pl.BlockSpec((None, D_PLE, d), lambda i: (layer, 0, 0), pipeline_mode=once),
        ],
        out_specs=out_specs,
        out_shape=out_shape,
        compiler_params=_params("parallel"),
        name="ln_ple_gate",
    )(x_res, y, p, g, b, wg, wp)


def _mlstm_kernel(q_ref, k_ref, v_ref, o_ref, g_ref, ib_ref, fb_ref, out_ref,
                  b_ref, r_ref, *state, chunk):
    L = chunk
    nck = q_ref.shape[1] // L
    c_refs, n_refs, m_refs = state[:HEADS], state[HEADS:2 * HEADS], state[2 * HEADS:]
    tn = (((0,), (0,)), ((), ()))
    nt = (((1,), (1,)), ((), ()))

    @pl.when(pl.program_id(1) == 0)
    def _():
        for h in range(HEADS):
            c_refs[h][...] = jnp.zeros((QK_DIM, ML_V_DIM), F32)
            n_refs[h][...] = jnp.zeros((1, QK_DIM), F32)
            m_refs[h][...] = jnp.zeros((1, 1), F32)

    upper = (_iota2((L, L), 0) <= _iota2((L, L), 1)).astype(BF16)
    for c in range(nck):
        gates = g_ref[0, c]
        logf = _log_sigmoid(gates[HEADS:] + fb_ref[...])
        b_c = _dot_f32_lhs(jnp.concatenate([logf, logf], axis=0), upper)[0:HEADS]
        b_ref[c] = b_c
        r_ref[c] = gates[:HEADS] + ib_ref[...] - b_c

    sub16 = _iota2((2 * SUBLANES, L), 0)
    sub_c = _iota2((2 * SUBLANES, 2 * LANES), 0)
    lane_c = _iota2((2 * SUBLANES, 2 * LANES), 1)
    rhs_const = jnp.where(((sub_c < 3) & (lane_c < LANES)) | ((sub_c >= 3) & (sub_c < 6) & (lane_c >= LANES)),
                          1.0, 0.0).astype(BF16)
    causal = _iota2((L, L), 1) <= _iota2((L, L), 0)

    def split_rows(x):
        hi = x.astype(BF16).astype(F32)
        rest = x - hi
        mid = rest.astype(BF16).astype(F32)
        return hi, mid, rest - mid

    def step(c, carry, first_head):
        r0 = c * L if isinstance(c, int) else pl.multiple_of(c * L, L)
        b_all = b_ref[c]
        r_all = r_ref[c]
        for h in range(first_head, first_head + ML_HEAD_GROUP):
            q = q_ref[0, pl.ds(r0, L), h * QK_DIM:(h + 1) * QK_DIM]
            k = k_ref[0, pl.ds(r0, L), h * QK_DIM:(h + 1) * QK_DIM]
            v = v_ref[0, pl.ds(r0, L), h * ML_V_DIM:(h + 1) * ML_V_DIM]
            b_row = b_all[h:h + 1, :]
            r_row = r_all[h:h + 1, :]
            bh, bm, bl = split_rows(b_row)
            rh, rm, rl = split_rows(r_row)
            lhs = jnp.where(sub16 == 0, bh, jnp.where(sub16 == 1, bm, jnp.where(sub16 == 2, bl,
                  jnp.where(sub16 == 3, rh, jnp.where(sub16 == 4, rm, jnp.where(sub16 == 5, rl,
                  jnp.where(sub16 < 9, 1.0, 0.0))))))).astype(BF16)
            rhs_d = jnp.where(sub16 < 3, 1.0, jnp.where(sub16 == 6, rh, jnp.where(sub16 == 7, rm,
                    jnp.where(sub16 == 8, rl, 0.0)))).astype(BF16)
            rhs = jnp.concatenate([rhs_d, rhs_const], axis=1)
            cols = lax.dot_general(lhs, rhs, tn, preferred_element_type=F32)
            b_col = cols[:, L:L + LANES]
            r_col = cols[:, L + LANES:]
            m_prev = m_refs[h][...]

            d_intra = jnp.where(causal, cols[:, 0:L], -jnp.inf)
            d_inter = b_col + m_prev
            m_t = jnp.maximum(d_inter, jnp.max(d_intra, axis=1, keepdims=True))
            w_intra = jnp.exp(d_intra - jnp.concatenate([m_t] * (L // LANES), axis=1))
            w_inter = jnp.exp(d_inter - m_t)
            s = lax.dot_general(q, k, nt, preferred_element_type=F32) * w_intra
            c_old = c_refs[h][...]
            n_old = n_refs[h][...]
            num = (jnp.concatenate([w_inter] * (ML_V_DIM // LANES), axis=1) * _dot(q, c_old.astype(BF16))
                   + _dot(s.astype(BF16), v))
            den = (w_inter[:, 0:1] * jnp.sum(q.astype(F32) * n_old, axis=1, keepdims=True)
                   + jnp.sum(s, axis=1, keepdims=True))
            hval = num / jnp.maximum(jnp.abs(den), jnp.exp(-m_t[:, 0:1]))
            gate = _sigmoid(o_ref[0, pl.ds(r0, L), h * ML_V_DIM:(h + 1) * ML_V_DIM].astype(F32))
            out_ref[0, pl.ds(r0, L), h * ML_V_DIM:(h + 1) * ML_V_DIM] = (gate * hval).astype(out_ref.dtype)

            b_last = b_row[:, L - 1:L]
            m_new = jnp.maximum(b_last + m_prev, jnp.max(b_last + r_row, axis=1, keepdims=True))
            decay = jnp.exp(b_last + m_prev - m_new)
            wk = jnp.exp(b_last + r_col - m_new) * k.astype(F32)
            c_refs[h][...] = decay * c_old + lax.dot_general(wk.astype(BF16), v, tn, preferred_element_type=F32)
            n_refs[h][...] = decay * n_old + jnp.sum(wk, axis=0, keepdims=True)
            m_refs[h][...] = m_new
        return carry

    for first_head in range(0, HEADS, ML_HEAD_GROUP):
        if ML_UNROLL_CHUNKS:
            for c in range(nck):
                step(c, 0, first_head)
        else:
            lax.fori_loop(0, nck, functools.partial(step, first_head=first_head), 0)


def _mlstm(z3, gate_rows, i_bias, f_bias):
    b, s, _ = z3.shape
    L = ML_CHUNK
    ts = ML_TS
    nck = ts // L
    qk_w = HEADS * QK_DIM
    gate_spec = pl.BlockSpec((1, nck, 2 * HEADS, L), lambda bi, si: (bi, si, 0, 0))
    bias_spec = pl.BlockSpec((HEADS, L), lambda bi, si: (0, 0))
    return pl.pallas_call(
        functools.partial(_mlstm_kernel, chunk=L),
        grid=(b, s // ts),
        in_specs=[
            pl.BlockSpec((1, ts, qk_w), lambda bi, si: (bi, si, 0)),
            pl.BlockSpec((1, ts, qk_w), lambda bi, si: (bi, si, 1)),
            pl.BlockSpec((1, ts, D_MODEL), lambda bi, si: (bi, si, 1)),
            pl.BlockSpec((1, ts, D_MODEL), lambda bi, si: (bi, si, 2)),
            gate_spec, bias_spec, bias_spec,
        ],
        out_specs=pl.BlockSpec((1, ts, D_MODEL), lambda bi, si: (bi, si, 0)),
        out_shape=jax.ShapeDtypeStruct((b, s, D_MODEL), BF16),
        scratch_shapes=(
            [pltpu.VMEM((nck, HEADS, L), F32)] * 2
            + [pltpu.VMEM((QK_DIM, ML_V_DIM), F32)] * HEADS
            + [pltpu.VMEM((1, QK_DIM), F32)] * HEADS
            + [pltpu.VMEM((1, 1), F32)] * HEADS),
        compiler_params=_params("parallel", "arbitrary"),
        name="mlstm",
    )(z3, z3, z3, z3, gate_rows, i_bias, f_bias)


def _col_scale(n, *ranges):
    col = jnp.arange(n)[None, :]
    scale = jnp.ones((1, n), F32)
    for lo, hi, value in ranges:
        scale = jnp.where((col >= lo) & (col < hi), value, scale)
    return scale


def _row(v):
    return v.astype(F32).reshape(1, -1)


def kernel(x, p, ev_w_in, ev_b_fgate, ev_dw_kernel, ev_dw_bias, ev_cnorm_g, ev_cnorm_b, ev_w_out,
           od_w_in, od_b_igate, od_b_fgate, od_w_out, ln_mix_g, ln_mix_b, w_up, w_down,
           ln_ffn_g, ln_ffn_b, w_ple, w_ple_gate):
    B, S, D = x.shape
    M = B * S
    x_f = x.reshape(M, D)
    p2 = p.reshape(DEPTH, M, D_PLE)
    w_up_b, w_down_b = w_up.astype(BF16), w_down.astype(BF16)
    w_ple_b, w_ple_gate_b = w_ple.astype(BF16), w_ple_gate.astype(BF16)

    def ffn_ple(layer, x_bf, x_res, with_bf16):
        y = _ffn(x_bf, w_up_b, w_down_b, layer)
        return _ln_ple(x_res, y, p2, _row(ln_ffn_g[layer]), _row(ln_ffn_b[layer]),
                       w_ple_gate_b, w_ple_b, layer, with_bf16)

    qkv_w = 3 * ATT_WIDTH
    n_main = ev_w_in.shape[2] - HEADS
    z, g_t = _in_proj(x_f, ev_w_in[0].T, qkv_w, HEADS, _col_scale(
        n_main, (0, ATT_WIDTH, QK_SCALE * LOG2E)))
    z3 = z.reshape(B, S, -1)
    f_bias = jnp.broadcast_to(ev_b_fgate[0].astype(F32)[:, None, None], (HEADS, 1, LANES))
    attn = _fox_attention(z3, g_t.reshape(GATE_PAD, B, S // LANES, LANES), f_bias)
    taps = jnp.broadcast_to(ev_dw_kernel[0].astype(F32)[:, None, :], (CONV_WIDTH, SUBLANES, CONV_CH))
    conv = _conformer_conv(z3, taps, jnp.broadcast_to(_row(ev_dw_bias[0]), (SUBLANES, CONV_CH)),
                           _row(ev_cnorm_g[0]), _row(ev_cnorm_b[0]))
    x_f, x_bf = _out_proj(attn.reshape(M, -1), 0, conv.reshape(M, -1), 0, ev_w_out[0].astype(BF16),
                          x_f, _row(ln_mix_g[0]), _row(ln_mix_b[0]))
    x_f, x_bf = ffn_ple(0, x_bf, x_f, True)

    qkv_w = 2 * HEADS * QK_DIM + D_MODEL
    n_main = od_w_in.shape[2] - 2 * HEADS
    z, g_t = _in_proj(x_bf, od_w_in[0].T, qkv_w, 2 * HEADS, _col_scale(
        n_main, (ATT_WIDTH, 2 * ATT_WIDTH, QK_SCALE)))
    z3 = z.reshape(B, S, -1)
    gate_rows = g_t[:2 * HEADS].reshape(2 * HEADS, B, S // ML_CHUNK, ML_CHUNK).transpose(1, 2, 0, 3)
    i_bias = jnp.broadcast_to(od_b_igate[0].astype(F32)[:, None], (HEADS, ML_CHUNK))
    f_bias = jnp.broadcast_to(od_b_fgate[0].astype(F32)[:, None], (HEADS, ML_CHUNK))
    hg = _mlstm(z3, gate_rows, i_bias, f_bias).reshape(M, D)
    x_f, x_bf = _out_proj(hg, 0, hg, 1, od_w_out[0].astype(BF16), x_f, _row(ln_mix_g[1]), _row(ln_mix_b[1]))
    (x_f,) = ffn_ple(1, x_bf, x_f, False)
    return x_f.reshape(B, S, D)
```

```python
import functools

import jax
import jax.numpy as jnp
from jax import lax
from jax.experimental import pallas as pl
from jax.experimental.pallas import tpu as pltpu

F32 = jnp.float32
BF16 = jnp.bfloat16

D_MODEL = 2048
DEPTH = 2
D_PLE = 256
D_FF = 4 * D_MODEL
LN_EPS = 1e-5
HEADS = 8
QK_DIM = 128
ATT_WIDTH = HEADS * QK_DIM
CONV_CH = D_MODEL - ATT_WIDTH
CONV_WIDTH = 31
ML_V_DIM = D_MODEL // HEADS
ALPHA = (2 * DEPTH) ** 0.25
QK_SCALE = QK_DIM ** -0.5
LOG2E = 1.4426950408889634

LANES = 128
SUBLANES = 8
GATE_PAD = LANES
CONV_HALO = 32
VMEM_LIMIT = 58 * 1024 * 1024

PROJ_TM, PROJ_TN = 2048, 1024
ATT_TQ = 4096
ATT_QW = 256
ATT_TK = 512
CONV_TS = 512
CONV_ROWS = 32
OUT_TM = 512
FFN_TM, FFN_TF, FFN_ROWS = 1024, 1024, 512
PLE_TM = 512
OUT_SPLIT = 4
PLE_SPLIT = 2
ML_CHUNK = 256
ML_TS = 1024


def _params(*sem):
    return pltpu.CompilerParams(dimension_semantics=sem, vmem_limit_bytes=VMEM_LIMIT)


def _layer_norm(t, g, b):
    mu = jnp.mean(t, axis=-1, keepdims=True)
    tc = t - mu
    var = jnp.mean(tc * tc, axis=-1, keepdims=True)
    return tc * lax.rsqrt(var + LN_EPS) * g + b


def _log_sigmoid(x):
    return -(jnp.maximum(-x, 0.0) + jnp.log1p(jnp.exp(-jnp.abs(x))))


def _sigmoid(x):
    return 1.0 / (1.0 + jnp.exp(-x))


def _split3(x):
    hi = x.astype(BF16)
    r1 = x - hi.astype(F32)
    mid = r1.astype(BF16)
    lo = (r1 - mid.astype(F32)).astype(BF16)
    return hi, mid, lo


def _dot(a, b):
    return jnp.dot(a, b, preferred_element_type=F32)


def _dot_f32_lhs(x, ones_rhs):
    hi, mid, lo = _split3(x)
    return _dot(hi, ones_rhs) + _dot(mid, ones_rhs) + _dot(lo, ones_rhs)


def _dot_f32_rhs(ones_lhs, x):
    hi, mid, lo = _split3(x)
    return _dot(ones_lhs, hi) + _dot(ones_lhs, mid) + _dot(ones_lhs, lo)


def _iota2(shape, axis):
    return lax.broadcasted_iota(jnp.int32, shape, axis)


def _proj_kernel(x_ref, wt_ref, wgt_ref, sc_ref, z_ref, gt_ref, *maybe_xb_ref):
    first = pl.program_id(1) == 0
    if maybe_xb_ref:
        (xb_ref,) = maybe_xb_ref

        @pl.when(first)
        def _():
            xb_ref[...] = x_ref[...].astype(BF16)
    else:
        xb_ref = x_ref
    nt = (((1,), (1,)), ((), ()))
    z = lax.dot_general(xb_ref[...], wt_ref[...].astype(BF16), nt, preferred_element_type=F32)
    z_ref[...] = (z * sc_ref[...]).astype(z_ref.dtype)

    @pl.when(first)
    def _():
        gt_ref[...] = lax.dot_general(wgt_ref[...].astype(BF16), xb_ref[...], nt, preferred_element_type=F32)


def _in_proj(x, w_t, gate_lo, n_gate, col_scale):
    m, k = x.shape
    n = w_t.shape[0] - n_gate
    tn = PROJ_TN
    assert gate_lo % tn == 0 and gate_lo % GATE_PAD == 0 and n % tn == 0

    def w_row(j):
        groups = j * (tn // SUBLANES) + jnp.where(j * tn >= gate_lo, n_gate // SUBLANES, 0)
        return groups * SUBLANES

    tm = PROJ_TM if x.dtype == BF16 else PROJ_TM // 2
    scratch = [] if x.dtype == BF16 else [pltpu.VMEM((tm, k), BF16)]
    return pl.pallas_call(
        _proj_kernel,
        grid=(m // tm, n // tn),
        in_specs=[
            pl.BlockSpec((tm, k), lambda i, j: (i, 0)),
            pl.BlockSpec((pl.Element(tn), pl.Element(k)), lambda i, j: (w_row(j), 0)),
            pl.BlockSpec((GATE_PAD, k), lambda i, j: (gate_lo // GATE_PAD, 0)),
            pl.BlockSpec((1, tn), lambda i, j: (0, j)),
        ],
        out_specs=[
            pl.BlockSpec((tm, tn), lambda i, j: (i, j)),
            pl.BlockSpec((GATE_PAD, tm), lambda i, j: (0, i)),
        ],
        out_shape=[
            jax.ShapeDtypeStruct((m, n), BF16),
            jax.ShapeDtypeStruct((GATE_PAD, m), F32),
        ],
        scratch_shapes=scratch,
        compiler_params=_params("parallel", "arbitrary"),
        name="in_proj",
    )(x, w_t, w_t, col_scale)


def _attn_kernel(q_ref, k_ref, v_ref, f_ref, fb_ref, o_ref, kaug_ref, vt_ref, s0_ref, s1_ref, *state, tq):
    s_refs = (s0_ref, s1_ref)
    i = pl.program_id(2)
    qw, tk = ATT_QW, ATT_TK
    nsub = tq // qw
    blocks_per_tile = tk // qw
    tiles_per_q = tq // tk
    assert tiles_per_q % 2 == 0
    m_refs, l_refs, acc_refs = state[:nsub], state[nsub:2 * nsub], state[2 * nsub:]
    n_grp = f_ref.shape[2]

    def prepare_keys_values():
        f_rows = jnp.concatenate(
            [f_ref[0, 0] + fb_ref[0], jnp.zeros((LANES - n_grp, LANES), F32)], axis=0)
        ft = f_rows.T[:, 0:n_grp]
        logf = _log_sigmoid(ft)
        lower = (_iota2((LANES, LANES), 1) <= _iota2((LANES, LANES), 0)).astype(BF16)
        within = _dot_f32_rhs(lower, logf)
        tot = jnp.broadcast_to(within[LANES - 1:LANES, :], (2 * SUBLANES, n_grp))
        before = (_iota2((n_grp, n_grp), 0) < _iota2((n_grp, n_grp), 1)).astype(BF16)
        cum = (within + _dot_f32_lhs(tot, before)[0:1, :]) * LOG2E
        lane = _iota2((LANES, LANES), 1)
        for r in range(n_grp):
            col = jnp.broadcast_to(cum[:, r:r + 1], (LANES, LANES))
            hi = col.astype(BF16).astype(F32)
            rest = col - hi
            mid = rest.astype(BF16).astype(F32)
            lo = rest - mid
            terms = jnp.where(lane == 0, hi, jnp.where(lane == 1, mid, jnp.where(lane == 2, lo, 0.0)))
            kaug_ref[r * LANES:(r + 1) * LANES, QK_DIM:] = terms.astype(BF16)
        kaug_ref[:, :QK_DIM] = k_ref[0]
        for blk in range(vt_ref.shape[0]):
            vt_ref[blk] = v_ref[0, blk * qw:(blk + 1) * qw, :].T

    pl.when(i == 0)(prepare_keys_values)

    q_aug =jnp.where(_iota2((qw, LANES), 1) < 3, -1.0, 0.0).astype(BF16)
    q_t = [jnp.concatenate([q_ref[0, u * qw:(u + 1) * qw, :], q_aug], axis=1).T for u in range(nsub)]
    for u in range(nsub):
        m_refs[u][...] = jnp.full((1, qw), -jnp.inf, F32)
        l_refs[u][...] = jnp.zeros((1, qw), F32)
        acc_refs[u][...] = jnp.zeros((QK_DIM, qw), F32)

    def scores(t, slot, subs):
        kk = kaug_ref[pl.ds(pl.multiple_of(t * tk, tk), tk), :]
        for u in subs:
            s_refs[slot][u] = _dot(kk, q_t[u])

    def update(t, slot, u, mask):
        s = s_refs[slot][u]
        if mask is not None:
            s = jnp.where(mask, s, -jnp.inf)
        m_old = m_refs[u][...]
        m_new = jnp.maximum(m_old, jnp.max(s, axis=0, keepdims=True))
        alpha = jnp.exp2(m_old - m_new)
        p = jnp.exp2(s - m_new)
        l_refs[u][...] = alpha * l_refs[u][...] + jnp.sum(p, axis=0, keepdims=True)
        pb = p.astype(BF16)
        pv = _dot(vt_ref[t * blocks_per_tile], pb[0:qw, :])
        for n in range(1, blocks_per_tile):
            pv = pv + _dot(vt_ref[t * blocks_per_tile + n], pb[n * qw:(n + 1) * qw, :])
        acc_refs[u][...] = alpha * acc_refs[u][...] + pv
        m_refs[u][...] = m_new

    n_main = i * tiles_per_q
    scores(0, 0, range(nsub))

    def body(jj, carry):
        for half in range(2):
            t = 2 * jj + half
            scores(t + 1, 1 - half, range(nsub))
            for u in range(nsub):
                update(t, half, u, None)
        return carry

    lax.fori_loop(0, n_main // 2, body, 0)
    for d in range(tiles_per_q):
        if d + 1 < tiles_per_q:
            scores(n_main + d + 1, (d + 1) % 2, [u for u in range(nsub) if (u + 1) * qw > (d + 1) * tk])
        for u in range(nsub):
            if (u + 1) * qw <= d * tk:
                continue
            mask = None
            if u * qw < (d + 1) * tk - 1:
                mask = d * tk + _iota2((tk, qw), 0) <= u * qw + _iota2((tk, qw), 1)
            update(n_main + d, d % 2, u, mask)
    for u in range(nsub):
        out = (acc_refs[u][...] / l_refs[u][...]).T
        o_ref[0, u * qw:(u + 1) * qw, :] = out.astype(o_ref.dtype)


def _fox_attention(z3, f_rows, f_bias):
    b, s, _ = z3.shape
    tq = ATT_TQ
    nsub = tq // ATT_QW
    return pl.pallas_call(
        functools.partial(_attn_kernel, tq=tq),
        grid=(b, HEADS, s // tq),
        in_specs=[
            pl.BlockSpec((1, tq, QK_DIM), lambda bi, h, i: (bi, i, h)),
            pl.BlockSpec((1, s, QK_DIM), lambda bi, h, i: (bi, 0, HEADS + h)),
            pl.BlockSpec((1, s, QK_DIM), lambda bi, h, i: (bi, 0, 2 * HEADS + h)),
            pl.BlockSpec((1, 1, s // LANES, LANES), lambda bi, h, i: (h, bi, 0, 0)),
            pl.BlockSpec((1, 1, LANES), lambda bi, h, i: (h, 0, 0)),
        ],
        out_specs=pl.BlockSpec((1, tq, QK_DIM), lambda bi, h, i: (bi, i, h)),
        out_shape=jax.ShapeDtypeStruct((b, s, ATT_WIDTH), BF16),
        scratch_shapes=(
            [pltpu.VMEM((s, 2 * QK_DIM), BF16),
             pltpu.VMEM((s // ATT_QW, QK_DIM, ATT_QW), BF16),
             pltpu.VMEM((nsub, ATT_TK, ATT_QW), F32),
             pltpu.VMEM((nsub, ATT_TK, ATT_QW), F32)]
            + [pltpu.VMEM((1, ATT_QW), F32)] * (2 * nsub)
            + [pltpu.VMEM((QK_DIM, ATT_QW), F32)] * nsub),
        compiler_params=_params("parallel", "parallel", "arbitrary"),
        name="fox_attention",
    )(z3, z3, z3, f_rows, f_bias)


def _conv_kernel(a_ref, g_ref, kw_ref, kb_ref, ng_ref, nb_ref, o_ref, ybuf, conv_ref, *, ts):
    si = pl.program_id(1)

    @pl.when(si == 0)
    def _():
        for r in range(SUBLANES):
            ybuf[r, 0:CONV_HALO - r, :] = jnp.zeros((CONV_HALO - r, CONV_CH), F32)

    @pl.when(si > 0)
    def _():
        for r in range(SUBLANES):
            ybuf[r, 0:CONV_HALO - r, :] = ybuf[r, ts:ts + CONV_HALO - r, :]

    y = a_ref[0].astype(F32) * _sigmoid(g_ref[0].astype(F32))
    for r in range(SUBLANES):
        ybuf[r, CONV_HALO - r:CONV_HALO - r + ts, :] = y

    first = CONV_HALO - (CONV_WIDTH - 1)
    rows = CONV_ROWS

    def block(rb, carry):
        r0 = pl.multiple_of(rb * rows, rows)
        acc = [kb_ref[...]] * (rows // SUBLANES)
        for w in range(CONV_WIDTH):
            q8, r = divmod(first + w, SUBLANES)
            tap = kw_ref[w]
            for sb in range(rows // SUBLANES):
                acc[sb] = acc[sb] + ybuf[r, pl.ds(r0 + (q8 + sb) * SUBLANES, SUBLANES), :] * tap
        conv_ref[pl.ds(r0, rows), :] = jnp.concatenate(acc, axis=0)
        return carry

    lax.fori_loop(0, ts // rows, block, 0)
    t = _layer_norm(conv_ref[...], ng_ref[...], nb_ref[...])
    o_ref[0] = (t * _sigmoid(t)).astype(o_ref.dtype)


def _conformer_conv(z3, kw, kb, ng, nb):
    b, s, _ = z3.shape
    ts = CONV_TS
    a_blk = 3 * ATT_WIDTH // CONV_CH
    vec = pl.BlockSpec((1, CONV_CH), lambda bi, si: (0, 0))
    return pl.pallas_call(
        functools.partial(_conv_kernel, ts=ts),
        grid=(b, s // ts),
        in_specs=[
            pl.BlockSpec((1, ts, CONV_CH), lambda bi, si: (bi, si, a_blk)),
            pl.BlockSpec((1, ts, CONV_CH), lambda bi, si: (bi, si, a_blk + 1)),
            pl.BlockSpec((CONV_WIDTH, SUBLANES, CONV_CH), lambda bi, si: (0, 0, 0)),
            pl.BlockSpec((SUBLANES, CONV_CH), lambda bi, si: (0, 0)),
            vec, vec,
        ],
        out_specs=pl.BlockSpec((1, ts, CONV_CH), lambda bi, si: (bi, si, 0)),
        out_shape=jax.ShapeDtypeStruct((b, s, CONV_CH), BF16),
        scratch_shapes=[pltpu.VMEM((SUBLANES, ts + CONV_HALO, CONV_CH), F32),
                        pltpu.VMEM((ts, CONV_CH), F32)],
        compiler_params=_params("parallel", "arbitrary"),
        name="conformer_conv",
    )(z3, z3, kw, kb, ng, nb)


def _out_kernel(a1_ref, a2_ref, w1_ref, w2_ref, xr_ref, g_ref, b_ref, of_ref, ob_ref):
    tm = of_ref.shape[0]
    for c in range(OUT_SPLIT):
        rows = slice(c * tm // OUT_SPLIT, (c + 1) * tm // OUT_SPLIT)
        y = _dot(a1_ref[rows, :], w1_ref[...]) + _dot(a2_ref[rows, :], w2_ref[...])
        x = _layer_norm(ALPHA * xr_ref[rows, :] + y, g_ref[...], b_ref[...])
        of_ref[rows, :] = x
        ob_ref[rows, :] = x.astype(BF16)


def _out_proj(a1, a1_blk, a2, a2_blk, w, x_res, g, b):
    m, d = x_res.shape
    half = d // 2
    tm = OUT_TM
    vec = pl.BlockSpec((1, d), lambda i: (0, 0))
    return pl.pallas_call(
        _out_kernel,
        grid=(m // tm,),
        in_specs=[
            pl.BlockSpec((tm, half), lambda i: (i, a1_blk)),
            pl.BlockSpec((tm, half), lambda i: (i, a2_blk)),
            pl.BlockSpec((half, d), lambda i: (0, 0)),
            pl.BlockSpec((half, d), lambda i: (1, 0)),
            pl.BlockSpec((tm, d), lambda i: (i, 0)),
            vec, vec,
        ],
        out_specs=[pl.BlockSpec((tm, d), lambda i: (i, 0)), pl.BlockSpec((tm, d), lambda i: (i, 0))],
        out_shape=[jax.ShapeDtypeStruct((m, d), F32), jax.ShapeDtypeStruct((m, d), BF16)],
        compiler_params=_params("parallel"),
        name="out_proj_ln",
    )(a1, a2, w, w, x_res, g, b)


def _ffn_kernel(x_ref, wu_ref, wd_ref, o_ref):
    @pl.when(pl.program_id(1) == 0)
    def _():
        o_ref[...] = jnp.zeros(o_ref.shape, F32)

    for c in range(o_ref.shape[0] // FFN_ROWS):
        rows = slice(c * FFN_ROWS, (c + 1) * FFN_ROWS)
        h = jnp.maximum(_dot(x_ref[rows, :], wu_ref[...]), 0.0)
        o_ref[rows, :] += _dot((h * h).astype(BF16), wd_ref[...])


def _ffn(x_bf, w_up, w_down, layer):
    m, d = x_bf.shape
    dff = w_up.shape[2]
    tm, tf = FFN_TM, FFN_TF
    return pl.pallas_call(
        _ffn_kernel,
        grid=(m // tm, dff // tf),
        in_specs=[
            pl.BlockSpec((tm, d), lambda i, f: (i, 0)),
            pl.BlockSpec((None, d, tf), lambda i, f: (layer, 0, f)),
            pl.BlockSpec((None, tf, d), lambda i, f: (layer, f, 0)),
        ],
        out_specs=pl.BlockSpec((tm, d), lambda i, f: (i, 0)),
        out_shape=jax.ShapeDtypeStruct((m, d), F32),
        compiler_params=_params("parallel", "arbitrary"),
        name="ffn",
    )(x_bf, w_up, w_down)


def _ple_kernel(xr_ref, y_ref, p_ref, g_ref, b_ref, wg_ref, wp_ref, of_ref, *maybe_ob_ref):
    tm = of_ref.shape[0]
    for c in range(PLE_SPLIT):
        rows = slice(c * tm // PLE_SPLIT, (c + 1) * tm // PLE_SPLIT)
        x = _layer_norm(ALPHA * xr_ref[rows, :] + y_ref[rows, :], g_ref[...], b_ref[...])
        gate = _sigmoid(_dot(x.astype(BF16), wg_ref[...]))
        out = x + gate * _dot(p_ref[rows, :].astype(BF16), wp_ref[...])
        of_ref[rows, :] = out
        for ob_ref in maybe_ob_ref:
            ob_ref[rows, :] = out.astype(BF16)


def _ln_ple(x_res, y, p, g, b, wg, wp, layer, with_bf16):
    m, d = x_res.shape
    tm = PLE_TM
    row = pl.BlockSpec((tm, d), lambda i: (i, 0))
    vec = pl.BlockSpec((1, d), lambda i: (0, 0))
    once = pl.Buffered(1)
    out_specs = [row]
    out_shape = [jax.ShapeDtypeStruct((m, d), F32)]
    if with_bf16:
        out_specs.append(row)
        out_shape.append(jax.ShapeDtypeStruct((m, d), BF16))
    return pl.pallas_call(
        _ple_kernel,
        grid=(m // tm,),
        in_specs=[
            row, row,
            pl.BlockSpec((None, tm, D_PLE), lambda i: (layer, i, 0)),
            vec, vec,
            pl.BlockSpec((None, d, d), lambda i: (layer, 0, 0), pipeline_mode=once),
            pl.BlockSpec((None, D_PLE, d), lambda i: (layer, 0, 0), pipeline_mode=once),
        ],
        out_specs=out_specs,
        out_shape=out_shape,
        compiler_params=_params("parallel"),
        name="ln_ple_gate",
    )(x_res, y, p, g, b, wg, wp)


def _mlstm_kernel(q_ref, k_ref, v_ref, o_ref, g_ref, ib_ref, fb_ref, out_ref,
                  b_ref, r_ref, *state, chunk):
    L = chunk
    nck = q_ref.shape[1] // L
    c_refs, n_refs, m_refs = state[:HEADS], state[HEADS:2 * HEADS], state[2 * HEADS:]
    tn = (((0,), (0,)), ((), ()))
    nt = (((1,), (1,)), ((), ()))

    @pl.when(pl.program_id(1) == 0)
    def _():
        for h in range(HEADS):
            c_refs[h][...] = jnp.zeros((QK_DIM, ML_V_DIM), F32)
            n_refs[h][...] = jnp.zeros((1, QK_DIM), F32)
            m_refs[h][...] = jnp.zeros((1, 1), F32)

    upper = (_iota2((L, L), 0) <= _iota2((L, L), 1)).astype(BF16)
    for c in range(nck):
        gates = g_ref[0, c]
        logf = _log_sigmoid(gates[HEADS:] + fb_ref[...])
        b_c = _dot_f32_lhs(jnp.concatenate([logf, logf], axis=0), upper)[0:HEADS]
        b_ref[c] = b_c
        r_ref[c] = gates[:HEADS] + ib_ref[...] - b_c

    sub16 = _iota2((2 * SUBLANES, L), 0)
    sub_c = _iota2((2 * SUBLANES, 2 * LANES), 0)
    lane_c = _iota2((2 * SUBLANES, 2 * LANES), 1)
    rhs_const = jnp.where(((sub_c < 3) & (lane_c < LANES)) | ((sub_c >= 3) & (sub_c < 6) & (lane_c >= LANES)),
                          1.0, 0.0).astype(BF16)
    causal = _iota2((L, L), 1) <= _iota2((L, L), 0)

    def split_rows(x):
        hi = x.astype(BF16).astype(F32)
        rest = x - hi
        mid = rest.astype(BF16).astype(F32)
        return hi, mid, rest - mid

    def step(c, carry):
        r0 = pl.multiple_of(c * L, L)
        b_all = b_ref[c]
        r_all = r_ref[c]
        for h in range(HEADS):
            q = q_ref[0, pl.ds(r0, L), h * QK_DIM:(h + 1) * QK_DIM]
            k = k_ref[0, pl.ds(r0, L), h * QK_DIM:(h + 1) * QK_DIM]
            v = v_ref[0, pl.ds(r0, L), h * ML_V_DIM:(h + 1) * ML_V_DIM]
            b_row = b_all[h:h + 1, :]
            r_row = r_all[h:h + 1, :]
            bh, bm, bl = split_rows(b_row)
            rh, rm, rl = split_rows(r_row)
            lhs = jnp.where(sub16 == 0, bh, jnp.where(sub16 == 1, bm, jnp.where(sub16 == 2, bl,
                  jnp.where(sub16 == 3, rh, jnp.where(sub16 == 4, rm, jnp.where(sub16 == 5, rl,
                  jnp.where(sub16 < 9, 1.0, 0.0))))))).astype(BF16)
            rhs_d = jnp.where(sub16 < 3, 1.0, jnp.where(sub16 == 6, rh, jnp.where(sub16 == 7, rm,
                    jnp.where(sub16 == 8, rl, 0.0)))).astype(BF16)
            rhs = jnp.concatenate([rhs_d, rhs_const], axis=1)
            cols = lax.dot_general(lhs, rhs, tn, preferred_element_type=F32)
            b_col = cols[:, L:L + LANES]
            r_col = cols[:, L + LANES:]
            m_prev = m_refs[h][...]

            d_intra = jnp.where(causal, cols[:, 0:L], -jnp.inf)
            d_inter = b_col + m_prev
            m_t = jnp.maximum(d_inter, jnp.max(d_intra, axis=1, keepdims=True))
            w_intra = jnp.exp(d_intra - jnp.concatenate([m_t] * (L // LANES), axis=1))
            w_inter = jnp.exp(d_inter - m_t)
            s = lax.dot_general(q, k, nt, preferred_element_type=F32) * w_intra
            c_old = c_refs[h][...]
            n_old = n_refs[h][...]
            num = (jnp.concatenate([w_inter] * (ML_V_DIM // LANES), axis=1) * _dot(q, c_old.astype(BF16))
                   + _dot(s.astype(BF16), v))
            den = (w_inter[:, 0:1] * jnp.sum(q.astype(F32) * n_old, axis=1, keepdims=True)
                   + jnp.sum(s, axis=1, keepdims=True))
            hval = num / jnp.maximum(jnp.abs(den), jnp.exp(-m_t[:, 0:1]))
            gate = _sigmoid(o_ref[0, pl.ds(r0, L), h * ML_V_DIM:(h + 1) * ML_V_DIM].astype(F32))
            out_ref[0, pl.ds(r0, L), h * ML_V_DIM:(h + 1) * ML_V_DIM] = (gate * hval).astype(out_ref.dtype)

            b_last = b_row[:, L - 1:L]
            m_new = jnp.maximum(b_last + m_prev, jnp.max(b_last + r_row, axis=1, keepdims=True))
            decay = jnp.exp(b_last + m_prev - m_new)
            wk = jnp.exp(b_last + r_col - m_new) * k.astype(F32)
            c_refs[h][...] = decay * c_old + lax.dot_general(wk.astype(BF16), v, tn, preferred_element_type=F32)
            n_refs[h][...] = decay * n_old + jnp.sum(wk, axis=0, keepdims=True)
            m_refs[h][...] = m_new
        return carry

    lax.fori_loop(0, nck, step, 0)


def _mlstm(z3, gate_rows, i_bias, f_bias):
    b, s, _ = z3.shape
    L = ML_CHUNK
    ts = ML_TS
    nck = ts // L
    qk_w = HEADS * QK_DIM
    gate_spec = pl.BlockSpec((1, nck, 2 * HEADS, L), lambda bi, si: (bi, si, 0, 0))
    bias_spec = pl.BlockSpec((HEADS, L), lambda bi, si: (0, 0))
    return pl.pallas_call(
        functools.partial(_mlstm_kernel, chunk=L),
        grid=(b, s // ts),
        in_specs=[
            pl.BlockSpec((1, ts, qk_w), lambda bi, si: (bi, si, 0)),
            pl.BlockSpec((1, ts, qk_w), lambda bi, si: (bi, si, 1)),
            pl.BlockSpec((1, ts, D_MODEL), lambda bi, si: (bi, si, 1)),
            pl.BlockSpec((1, ts, D_MODEL), lambda bi, si: (bi, si, 2)),
            gate_spec, bias_spec, bias_spec,
        ],
        out_specs=pl.BlockSpec((1, ts, D_MODEL), lambda bi, si: (bi, si, 0)),
        out_shape=jax.ShapeDtypeStruct((b, s, D_MODEL), BF16),
        scratch_shapes=(
            [pltpu.VMEM((nck, HEADS, L), F32)] * 2
            + [pltpu.VMEM((QK_DIM, ML_V_DIM), F32)] * HEADS
            + [pltpu.VMEM((1, QK_DIM), F32)] * HEADS
            + [pltpu.VMEM((1, 1), F32)] * HEADS),
        compiler_params=_params("parallel", "arbitrary"),
        name="mlstm",
    )(z3, z3, z3, z3, gate_rows, i_bias, f_bias)


def _col_scale(n, *ranges):
    col = jnp.arange(n)[None, :]
    scale = jnp.ones((1, n), F32)
    for lo, hi, value in ranges:
        scale = jnp.where((col >= lo) & (col < hi), value, scale)
    return scale


def _row(v):
    return v.astype(F32).reshape(1, -1)


def kernel(x, p, ev_w_in, ev_b_fgate, ev_dw_kernel, ev_dw_bias, ev_cnorm_g, ev_cnorm_b, ev_w_out,
           od_w_in, od_b_igate, od_b_fgate, od_w_out, ln_mix_g, ln_mix_b, w_up, w_down,
           ln_ffn_g, ln_ffn_b, w_ple, w_ple_gate):
    B, S, D = x.shape
    M = B * S
    x_f = x.reshape(M, D)
    p2 = p.reshape(DEPTH, M, D_PLE)
    w_up_b, w_down_b = w_up.astype(BF16), w_down.astype(BF16)
    w_ple_b, w_ple_gate_b = w_ple.astype(BF16), w_ple_gate.astype(BF16)

    def ffn_ple(layer, x_bf, x_res, with_bf16):
        y = _ffn(x_bf, w_up_b, w_down_b, layer)
        return _ln_ple(x_res, y, p2, _row(ln_ffn_g[layer]), _row(ln_ffn_b[layer]),
                       w_ple_gate_b, w_ple_b, layer, with_bf16)

    qkv_w = 3 * ATT_WIDTH
    n_main = ev_w_in.shape[2] - HEADS
    z, g_t = _in_proj(x_f, ev_w_in[0].T, qkv_w, HEADS, _col_scale(
        n_main, (0, ATT_WIDTH, QK_SCALE * LOG2E)))
    z3 = z.reshape(B, S, -1)
    f_bias = jnp.broadcast_to(ev_b_fgate[0].astype(F32)[:, None, None], (HEADS, 1, LANES))
    attn = _fox_attention(z3, g_t.reshape(GATE_PAD, B, S // LANES, LANES), f_bias)
    taps = jnp.broadcast_to(ev_dw_kernel[0].astype(F32)[:, None, :], (CONV_WIDTH, SUBLANES, CONV_CH))
    conv = _conformer_conv(z3, taps, jnp.broadcast_to(_row(ev_dw_bias[0]), (SUBLANES, CONV_CH)),
                           _row(ev_cnorm_g[0]), _row(ev_cnorm_b[0]))
    x_f, x_bf = _out_proj(attn.reshape(M, -1), 0, conv.reshape(M, -1), 0, ev_w_out[0].astype(BF16),
                          x_f, _row(ln_mix_g[0]), _row(ln_mix_b[0]))
    x_f, x_bf = ffn_ple(0, x_bf, x_f, True)

    qkv_w = 2 * HEADS * QK_DIM + D_MODEL
    n_main = od_w_in.shape[2] - 2 * HEADS
    z, g_t = _in_proj(x_bf, od_w_in[0].T, qkv_w, 2 * HEADS, _col_scale(
        n_main, (ATT_WIDTH, 2 * ATT_WIDTH, QK_SCALE)))
    z3 = z.reshape(B, S, -1)
    gate_rows = g_t[:2 * HEADS].reshape(2 * HEADS, B, S // ML_CHUNK, ML_CHUNK).transpose(1, 2, 0, 3)
    i_bias = jnp.broadcast_to(od_b_igate[0].astype(F32)[:, None], (HEADS, ML_CHUNK))
    f_bias = jnp.broadcast_to(od_b_fgate[0].astype(F32)[:, None], (HEADS, ML_CHUNK))
    hg = _mlstm(z3, gate_rows, i_bias, f_bias).reshape(M, D)
    x_f, x_bf = _out_proj(hg, 0, hg, 1, od_w_out[0].astype(BF16), x_f, _row(ln_mix_g[1]), _row(ln_mix_b[1]))
    (x_f,) = ffn_ple(1, x_bf, x_f, False)
    return x_f.reshape(B, S, D)
```
